```python
import math
import jax
import jax.numpy as jnp
from jax import lax
import numpy as np

D_MODEL = 2048
BATCH = 2
SEQ = 8192
DEPTH = 2
DEC_BATCH = 32
DEC_SEQ = 32
PAST_LEN = 2048

CHUNK = 64
Q_BLOCK = 128
HEAD_DIM = 128
N_MIX_HEADS = D_MODEL // HEAD_DIM
H_C = N_MIX_HEADS // 4
H_A = (N_MIX_HEADS - H_C) // 2
H_B = N_MIX_HEADS - H_A - H_C
MIX_WIDTH = N_MIX_HEADS * HEAD_DIM
DC = HEAD_DIM // 2
H_IDX = 16
D_IDX = 64
IDX_SCALE = (H_IDX * D_IDX) ** -0.5
IDX_TOPK_MAX = 256
N_BUCKETS = 32
MAX_DISTANCE = 256
H_BIAS = H_B + H_C
H_MEM = 4
N_MEM = 256
PEER_HEADS = 8
N_KEYS = 128
N_EXPERTS = N_KEYS * N_KEYS
D_KEY = 256
PEER_TOPK = 16
PEER_BLOCK = 128
N_IN = 3 * H_A * HEAD_DIM + H_A + 3 * H_B * HEAD_DIM + H_IDX * D_IDX + D_IDX + H_IDX + 3 * H_C * HEAD_DIM
ALPHA = (2 * DEPTH) ** 0.25
BETA = (8 * DEPTH) ** -0.25
LN_EPS = 1e-5
F32 = jnp.float32

kernel_name = 'hybrid_streaming_encoder_step'


def layer_norm(x, g, b):
    xf = x.astype(F32)
    mu = jnp.mean(xf, -1, keepdims=True)
    var = jnp.mean(jnp.square(xf - mu), -1, keepdims=True)
    return ((xf - mu) * lax.rsqrt(var + LN_EPS) * g.astype(F32) + b.astype(F32)).astype(x.dtype)


def rel_bucket(rel):
    half = N_BUCKETS // 2
    exact = half // 2
    base = jnp.where(rel > 0, half, 0)
    n = jnp.abs(rel)
    nf = jnp.maximum(n, 1).astype(F32)
    large = exact + (jnp.log(nf / exact) / math.log(MAX_DISTANCE / exact) * (half - exact)).astype(jnp.int32)
    large = jnp.minimum(large, half - 1)
    return base + jnp.where(n < exact, n, large)


def split_offsets():
    a, b, c = H_A * HEAD_DIM, H_B * HEAD_DIM, H_C * HEAD_DIM
    sizes = (a, a, a, H_A, b, b, b, H_IDX * D_IDX, D_IDX, H_IDX, c, c, c)
    offs, acc = [], 0
    for s in sizes[:-1]:
        acc += s
        offs.append(acc)
    return offs


def project(x, w_in, b_f):
    b, t, _ = x.shape
    (qa, ka, va, fa, qb, kb, vb, iq, ik, iw, qc, kc, vc) = jnp.split(x @ w_in, split_offsets(), axis=-1)
    heads = lambda a, n: a.reshape(b, t, n, -1)
    logf = jax.nn.log_sigmoid(fa.astype(F32) + b_f.astype(F32))
    return (heads(qa, H_A), heads(ka, H_A), heads(va, H_A), logf,
            heads(qb, H_B), heads(kb, H_B), heads(vb, H_B), heads(iq, H_IDX), ik, iw,
            heads(qc, H_C), heads(kc, H_C), heads(vc, H_C))


def fox_core(q, k, v, dq, dk, qpos, kpos):
    s = jnp.einsum('bqhd,bkhd->bhqk', q, k).astype(F32) * HEAD_DIM ** -0.5
    s = s + (jnp.swapaxes(dq, 1, 2)[..., :, None] - jnp.swapaxes(dk, 1, 2)[..., None, :])
    mask = kpos[None, :] <= qpos[:, None]
    p = jax.nn.softmax(jnp.where(mask, s, -jnp.inf), axis=-1).astype(v.dtype)
    return jnp.einsum('bhqk,bkhd->bqhd', p, v)


def dsa_core(q, iq, iw, k, v, ik, qpos, kpos, table_b, topk):
    b = q.shape[0]
    sc = jax.nn.relu(jnp.einsum('bqhd,bkd->bqhk', iq, ik).astype(F32))
    score = jnp.einsum('bqh,bqhk->bqk', iw.astype(F32) * IDX_SCALE, sc)
    adm = (kpos[None, :] // CHUNK) <= (qpos[:, None] // CHUNK)
    score = jnp.where(adm[None], score, -jnp.inf)
    top_s, idx = lax.top_k(score, topk)
    bi = jnp.arange(b)[:, None, None]
    kg = k[bi, idx]
    vg = v[bi, idx]
    s = jnp.einsum('bqhd,bqkhd->bhqk', q, kg).astype(F32) * HEAD_DIM ** -0.5
    bias = table_b[rel_bucket(kpos[idx] - qpos[None, :, None])]
    s = s + jnp.moveaxis(bias, -1, 1).astype(F32)
    s = jnp.where(jnp.isfinite(top_s)[:, None], s, -jnp.inf)
    p = jax.nn.softmax(s, axis=-1).astype(v.dtype)
    return jnp.einsum('bhqk,bqkhd->bqhd', p, vg)


def diff_core(q, k, v, qpos, kpos, table_c, lam, lam_init, g_sub):
    bias = jnp.transpose(table_c[rel_bucket(kpos[None, :] - qpos[:, None])], (2, 0, 1)).astype(F32)
    mask = (kpos[None, :] // CHUNK) <= (qpos[:, None] // CHUNK)

    def attn_map(qi, ki):
        s = jnp.einsum('bqhd,bkhd->bhqk', qi, ki).astype(F32) * DC ** -0.5 + bias
        return jax.nn.softmax(jnp.where(mask, s, -jnp.inf), axis=-1)

    p = attn_map(q[..., :DC], k[..., :DC]) - lam * attn_map(q[..., DC:], k[..., DC:])
    o = jnp.einsum('bhqk,bkhd->bqhd', p.astype(v.dtype), v).astype(F32)
    o = o * lax.rsqrt(jnp.mean(o * o, -1, keepdims=True) + LN_EPS) * g_sub.astype(F32) * (1.0 - lam_init)
    return o.astype(v.dtype)


def lambda_value(lam_p, layer_idx):
    lam_init = 0.8 - 0.6 * math.exp(-0.3 * layer_idx)
    lp = lam_p.astype(F32)
    lam = jnp.exp(jnp.sum(lp[0] * lp[1])) - jnp.exp(jnp.sum(lp[2] * lp[3])) + lam_init
    return lam, lam_init


def flat_heads(o):
    return o.reshape(o.shape[0], o.shape[1], -1)


def to_blocks(a):
    b, t = a.shape[:2]
    return jnp.moveaxis(a.reshape((b, t // Q_BLOCK, Q_BLOCK) + a.shape[2:]), 1, 0)


def from_blocks(a):
    nb, b, qb = a.shape[:3]
    return jnp.moveaxis(a, 0, 1).reshape((b, nb * qb) + a.shape[3:])


def mixer_prompt(x, w_in, b_f, w_out, lam, lam_init, g_sub, rel_bias):
    b, t, _ = x.shape
    (qa, ka, va, logf, qb, kb, vb, iq, ik, iw, qc, kc, vc) = project(x, w_in, b_f)
    dcum = jnp.cumsum(logf, axis=1)
    pos = jnp.arange(t)
    topk = min(IDX_TOPK_MAX, t // 4)
    table_b, table_c = rel_bias[:, :H_B], rel_bias[:, H_B:]

    def block(args):
        qa_b, dq_b, qb_b, iq_b, iw_b, qc_b, qpos = args
        oa = fox_core(qa_b, ka, va, dq_b, dcum, qpos, pos)
        ob = dsa_core(qb_b, iq_b, iw_b, kb, vb, ik, qpos, pos, table_b, topk)
        oc = diff_core(qc_b, kc, vc, qpos, pos, table_c, lam, lam_init, g_sub)
        return jnp.concatenate([flat_heads(oa), flat_heads(ob), flat_heads(oc)], axis=-1)

    o = lax.map(block, (to_blocks(qa), to_blocks(dcum), to_blocks(qb), to_blocks(iq), to_blocks(iw),
                        to_blocks(qc), pos.reshape(-1, Q_BLOCK)))
    y = from_blocks(o) @ w_out
    return y, (ka, va, logf, kb, vb, ik, kc, vc)


def mixer_sample(x, ca_k, ca_v, ca_logf, cb_k, cb_v, cb_ik, cc_k, cc_v,
                 w_in, b_f, w_out, lam, lam_init, g_sub, rel_bias):
    b, t, _ = x.shape
    p_len = ca_k.shape[1]
    (qa, ka, va, logf, qb, kb, vb, iq, ik, iw, qc, kc, vc) = project(x, w_in, b_f)
    cat = lambda c, n: jnp.concatenate([c.astype(n.dtype), n], axis=1)
    dcum = jnp.cumsum(jnp.concatenate([ca_logf.astype(F32), logf], axis=1), axis=1)
    kpos = jnp.arange(p_len + t)
    qpos = p_len + jnp.arange(t)
    topk = min(IDX_TOPK_MAX, (p_len + t) // 4)
    table_b, table_c = rel_bias[:, :H_B], rel_bias[:, H_B:]
    oa = fox_core(qa, cat(ca_k, ka), cat(ca_v, va), dcum[:, p_len:], dcum, qpos, kpos)
    ob = lax.map(lambda a: dsa_core(a[0][None], a[1][None], a[2][None], a[3][None], a[4][None], a[5][None],
                                    qpos, kpos, table_b, topk)[0],
                 (qb, iq, iw, cat(cb_k, kb), cat(cb_v, vb), cat(cb_ik, ik)))
    oc = diff_core(qc, cat(cc_k, kc), cat(cc_v, vc), qpos, kpos, table_c, lam, lam_init, g_sub)
    y = jnp.concatenate([flat_heads(oa), flat_heads(ob), flat_heads(oc)], axis=-1) @ w_out
    return y, (ka, va, logf, kb, vb, ik, kc, vc)


def mem_attend(x, mk, mv, w_mq, w_mo):
    b, t, _ = x.shape
    q = (x @ w_mq).reshape(b, t, H_MEM, HEAD_DIM)
    s = jnp.einsum('bqhd,bkhd->bhqk', q, mk).astype(F32) * HEAD_DIM ** -0.5
    p = jax.nn.softmax(s, axis=-1).astype(mv.dtype)
    o = jnp.einsum('bhqk,bkhd->bqhd', p, mv).reshape(b, t, H_MEM * HEAD_DIM)
    return o @ w_mo


def peer_block(xb, w_pq, sub_keys, u_tab, v_tab):
    tb = xb.shape[0]
    q = (xb @ w_pq).reshape(tb, PEER_HEADS, 2, D_KEY // 2)
    s = jnp.einsum('thcd,hcnd->thcn', q, sub_keys).astype(F32)
    sv, si = lax.top_k(s, PEER_TOPK)
    cand = (sv[:, :, 0, :, None] + sv[:, :, 1, None, :]).reshape(tb, PEER_HEADS, PEER_TOPK * PEER_TOPK)
    cidx = (si[:, :, 0, :, None] * N_KEYS + si[:, :, 1, None, :]).reshape(tb, PEER_HEADS, PEER_TOPK * PEER_TOPK)
    fv, fi = lax.top_k(cand, PEER_TOPK)
    eidx = jnp.take_along_axis(cidx, fi, axis=-1)
    gate = jax.nn.softmax(fv, axis=-1)
    act = jax.nn.gelu(jnp.einsum('td,thed->the', xb, u_tab[eidx]).astype(F32), approximate=False)
    return jnp.einsum('the,thed->td', (gate * act).astype(xb.dtype), v_tab[eidx])


def peer(x, w_pq, sub_keys, u_tab, v_tab):
    b, t, d = x.shape
    n = b * t
    nb = -(-n // PEER_BLOCK)
    xf = jnp.pad(x.reshape(n, d), ((0, nb * PEER_BLOCK - n), (0, 0)))
    out = lax.map(lambda xb: peer_block(xb, w_pq, sub_keys, u_tab, v_tab), xf.reshape(nb, PEER_BLOCK, d))
    return out.reshape(nb * PEER_BLOCK, d)[:n].reshape(b, t, d)


def setup_inputs(seed: int = 0) -> dict:
    key = jax.random.key(seed)
    ks = iter(jax.random.split(key, 48))

    def nrm(shape, scale):
        return jax.random.normal(next(ks), shape, F32) * scale

    kv_a = (DEPTH, DEC_BATCH, PAST_LEN, H_A, HEAD_DIM)
    kv_b = (DEPTH, DEC_BATCH, PAST_LEN, H_B, HEAD_DIM)
    kv_c = (DEPTH, DEC_BATCH, PAST_LEN, H_C, HEAD_DIM)
    kv_m = (DEPTH, DEC_BATCH, N_MEM, H_MEM, HEAD_DIM)
    dvec = (DEPTH, D_MODEL)
    return {
        'x_prompt': nrm((BATCH, SEQ, D_MODEL), 1.0),
        'x_sample': nrm((DEC_BATCH, DEC_SEQ, D_MODEL), 1.0),
        'cache_a_k': nrm(kv_a, 1.0),
        'cache_a_v': nrm(kv_a, 1.0),
        'cache_a_logf': jax.nn.log_sigmoid(2.0 + nrm((DEPTH, DEC_BATCH, PAST_LEN, H_A), 0.5)),
        'cache_b_k': nrm(kv_b, 1.0),
        'cache_b_v': nrm(kv_b, 1.0),
        'cache_b_ik': nrm((DEPTH, DEC_BATCH, PAST_LEN, D_IDX), 1.0),
        'cache_c_k': nrm(kv_c, 1.0),
        'cache_c_v': nrm(kv_c, 1.0),
        'cache_mem_k': nrm(kv_m, 1.0),
        'cache_mem_v': nrm(kv_m, 1.0),
        'mem_prompt': nrm((BATCH, N_MEM, D_MODEL), 1.0),
        'ln_in_g': 1.0 + nrm((D_MODEL,), 0.02),
        'ln_in_b': nrm((D_MODEL,), 0.02),
        'w_in': nrm((DEPTH, D_MODEL, N_IN), D_MODEL ** -0.5),
        'b_f': 2.0 + nrm((DEPTH, H_A), 0.5),
        'w_out': nrm((DEPTH, MIX_WIDTH, D_MODEL), BETA * MIX_WIDTH ** -0.5),
        'diff_lambda': nrm((DEPTH, 4, DC), 0.1),
        'diff_subln': 1.0 + nrm((DEPTH, HEAD_DIM), 0.02),
        'rel_bias': nrm((N_BUCKETS, H_BIAS), 0.5),
        'ln1_g': 1.0 + nrm(dvec, 0.02),
        'ln1_b': nrm(dvec, 0.02),
        'w_mq': nrm((DEPTH, D_MODEL, H_MEM * HEAD_DIM), D_MODEL ** -0.5),
        'w_mk': nrm((DEPTH, D_MODEL, H_MEM * HEAD_DIM), D_MODEL ** -0.5),
        'w_mv': nrm((DEPTH, D_MODEL, H_MEM * HEAD_DIM), D_MODEL ** -0.5),
        'w_mo': nrm((DEPTH, H_MEM * HEAD_DIM, D_MODEL), BETA * (H_MEM * HEAD_DIM) ** -0.5),
        'ln2_g': 1.0 + nrm(dvec, 0.02),
        'ln2_b': nrm(dvec, 0.02),
        'w_pq': nrm((DEPTH, D_MODEL, PEER_HEADS * D_KEY), D_MODEL ** -0.5),
        'sub_keys': nrm((DEPTH, PEER_HEADS, 2, N_KEYS, D_KEY // 2), (D_KEY // 2) ** -0.5),
        'u_tab': nrm((DEPTH, N_EXPERTS, D_MODEL), D_MODEL ** -0.5),
        'v_tab': nrm((DEPTH, N_EXPERTS, D_MODEL), BETA * PEER_HEADS ** -0.5),
        'ln3_g': 1.0 + nrm(dvec, 0.02),
        'ln3_b': nrm(dvec, 0.02),
    }


def reference(x_prompt, x_sample, cache_a_k, cache_a_v, cache_a_logf, cache_b_k, cache_b_v, cache_b_ik,
              cache_c_k, cache_c_v, cache_mem_k, cache_mem_v, mem_prompt, ln_in_g, ln_in_b, w_in, b_f, w_out,
              diff_lambda, diff_subln, rel_bias, ln1_g, ln1_b, w_mq, w_mk, w_mv, w_mo, ln2_g, ln2_b,
              w_pq, sub_keys, u_tab, v_tab, ln3_g, ln3_b):
    xp = layer_norm(x_prompt, ln_in_g, ln_in_b)
    xs = layer_norm(x_sample, ln_in_g, ln_in_b)
    bp = mem_prompt.shape[0]
    p_rows = [[] for _ in range(8)]
    s_rows = [[] for _ in range(8)]
    p_mk, p_mv = [], []
    for l in range(DEPTH):
        lam, lam_init = lambda_value(diff_lambda[l], l)
        y, rows = mixer_prompt(xp, w_in[l], b_f[l], w_out[l], lam, lam_init, diff_subln[l], rel_bias)
        xp = layer_norm(ALPHA * xp + y, ln1_g[l], ln1_b[l])
        mk = (mem_prompt @ w_mk[l]).reshape(bp, N_MEM, H_MEM, HEAD_DIM)
        mv = (mem_prompt @ w_mv[l]).reshape(bp, N_MEM, H_MEM, HEAD_DIM)
        xp = layer_norm(ALPHA * xp + mem_attend(xp, mk, mv, w_mq[l], w_mo[l]), ln2_g[l], ln2_b[l])
        xp = layer_norm(ALPHA * xp + peer(xp, w_pq[l], sub_keys[l], u_tab[l], v_tab[l]), ln3_g[l], ln3_b[l])
        for i in range(8):
            p_rows[i].append(rows[i])
        p_mk.append(mk)
        p_mv.append(mv)
        y, rows = mixer_sample(xs, cache_a_k[l], cache_a_v[l], cache_a_logf[l], cache_b_k[l], cache_b_v[l],
                               cache_b_ik[l], cache_c_k[l], cache_c_v[l], w_in[l], b_f[l], w_out[l],
                               lam, lam_init, diff_subln[l], rel_bias)
        xs = layer_norm(ALPHA * xs + y, ln1_g[l], ln1_b[l])
        xs = layer_norm(ALPHA * xs + mem_attend(xs, cache_mem_k[l], cache_mem_v[l], w_mq[l], w_mo[l]),
                        ln2_g[l], ln2_b[l])
        xs = layer_norm(ALPHA * xs + peer(xs, w_pq[l], sub_keys[l], u_tab[l], v_tab[l]), ln3_g[l], ln3_b[l])
        for i in range(8):
            s_rows[i].append(rows[i])
    ps = [jnp.stack(r) for r in p_rows]
    ss = [jnp.stack(r) for r in s_rows]
    return (xp, xs, ps[0], ps[1], ps[2], ps[3], ps[4], ps[5], ps[6], ps[7], jnp.stack(p_mk), jnp.stack(p_mv),
            ss[0], ss[1], ss[2], ss[3], ss[4], ss[5], ss[6], ss[7])
```

```python
import functools
import math

import jax
import jax.numpy as jnp
import numpy as np
from jax import lax
from jax.experimental import pallas as pl
from jax.experimental.pallas import tpu as pltpu

F32 = jnp.float32
BF16 = jnp.bfloat16

HEAD_DIM = 128
CHUNK = 64
CHUNK_SHIFT = 6
H_A, H_B, H_C = 6, 6, 4
DC = HEAD_DIM // 2
H_IDX, D_IDX = 16, 64
IDX_SCALE = (H_IDX * D_IDX) ** -0.5
IDX_TOPK_MAX = 256
N_BUCKETS = 32
MAX_DISTANCE = 256
H_MEM = 4
PEER_HEADS = 8
N_KEYS = 128
PEER_TOPK = 16
LN_EPS = 1e-5
NEG = -1e30

VMEM_LIMIT = 56 * 1024 * 1024
ATT_BLOCK = 512
SCORE_ROWS = 128
SCORE_COLS = 512
PEER_SEL_TOKENS = 256
PEER_TILE = 8
N_PICK = PEER_HEADS * PEER_TOPK


def _cparams(sem):
    return pltpu.CompilerParams(dimension_semantics=sem, vmem_limit_bytes=VMEM_LIMIT)


def _ln_kernel(*refs, alpha, has_y):
    if has_y:
        x_ref, y_ref, g_ref, b_ref, o32_ref, o16_ref = refs
        z = alpha * x_ref[...] + y_ref[...]
    else:
        x_ref, g_ref, b_ref, o32_ref, o16_ref = refs
        z = x_ref[...]
    mu = jnp.mean(z, axis=-1, keepdims=True)
    zc = z - mu
    var = jnp.mean(zc * zc, axis=-1, keepdims=True)
    o = zc * lax.rsqrt(var + LN_EPS) * g_ref[...] + b_ref[...]
    o32_ref[...] = o
    o16_ref[...] = o.astype(BF16)


def _ln_call(x, y, g, b, alpha):
    n, d = x.shape
    tm = 256 if n % 256 == 0 else n
    row = pl.BlockSpec((tm, d), lambda i: (i, 0))
    vec = pl.BlockSpec((1, d), lambda i: (0, 0))
    has_y = y is not None
    args = (x, y) if has_y else (x,)
    return pl.pallas_call(
        functools.partial(_ln_kernel, alpha=alpha, has_y=has_y),
        out_shape=(jax.ShapeDtypeStruct((n, d), F32), jax.ShapeDtypeStruct((n, d), BF16)),
        grid=(n // tm,),
        in_specs=[row] * len(args) + [vec, vec],
        out_specs=(row, row),
        compiler_params=_cparams(("parallel",)),
        name="layer_norm",
    )(*args, g.reshape(1, d), b.reshape(1, d))


def _matmul_kernel(x_ref, w_ref, o_ref):
    o_ref[...] = jnp.dot(x_ref[...], w_ref[...], preferred_element_type=F32)


def _pick(n, cands):
    for c in cands:
        if n % c == 0:
            return c
    return n


def _matmul(x, w):
    m, k = x.shape
    n = w.shape[1]
    tm = _pick(m, (512, 256, 128))
    tn = _pick(n, (1024, 512, 256, 128))
    return pl.pallas_call(
        _matmul_kernel,
        out_shape=jax.ShapeDtypeStruct((m, n), F32),
        grid=(n // tn, m // tm),
        in_specs=[pl.BlockSpec((tm, k), lambda j, i: (i, 0)),
                  pl.BlockSpec((k, tn), lambda j, i: (0, j))],
        out_specs=pl.BlockSpec((tm, tn), lambda j, i: (i, j)),
        compiler_params=_cparams(("parallel", "parallel")),
        name="projection",
    )(x, w)


def _softmax_init(m_sc, l_sc, acc_sc):
    m_sc[...] = jnp.full(m_sc.shape, NEG, F32)
    l_sc[...] = jnp.zeros(l_sc.shape, F32)
    acc_sc[...] = jnp.zeros(acc_sc.shape, F32)


def _softmax_step(s, v, m_ref, l_ref, acc_ref):
    m_prev = m_ref[...]
    m_new = jnp.maximum(m_prev, jnp.max(s, axis=-1, keepdims=True))
    a = jnp.exp(m_prev - m_new)
    p = jnp.exp(s - m_new)
    l_ref[...] = a * l_ref[...] + jnp.sum(p, axis=-1, keepdims=True)
    acc_ref[...] = a * acc_ref[...] + jnp.dot(p.astype(BF16), v, preferred_element_type=F32)
    m_ref[...] = m_new


def _qk(q, k):
    return lax.dot_general(q, k, (((1,), (1,)), ((), ())), preferred_element_type=F32)


def _flash_kernel(*refs, scale, causal, nk, has_bias, has_decay):
    refs = list(refs)
    q_ref, k_ref, v_ref = refs[:3]
    pos = 3
    bias_ref = dq_ref = dk_ref = None
    if has_bias:
        bias_ref = refs[pos]
        pos += 1
    if has_decay:
        dq_ref, dk_ref = refs[pos:pos + 2]
        pos += 2
    o_ref, m_sc, l_sc, acc_sc = refs[pos:pos + 4]
    i, j = pl.program_id(2), pl.program_id(3)
    last = i if causal else nk - 1

    @pl.when(j == 0)
    def _():
        _softmax_init(m_sc, l_sc, acc_sc)

    @pl.when(j <= last)
    def _():
        s = _qk(q_ref[0], k_ref[0]) * scale
        if has_bias:
            s = s + bias_ref[0, 0]
        if has_decay:
            s = s + (dq_ref[0, 0] - dk_ref[0, 0])
        _softmax_step(s, v_ref[0], m_sc, l_sc, acc_sc)

    @pl.when(j == last)
    def _():
        o_ref[0] = (acc_sc[...] / l_sc[...]).astype(o_ref.dtype)


def _flash_call(q, k, v, n_heads, tq, tk, scale, causal, bias=None, decay=None):
    b, t_q, _ = q.shape
    t_k = k.shape[1]
    nq, nk = t_q // tq, t_k // tk
    d = HEAD_DIM
    kj = (lambda i, j: jnp.minimum(j, i)) if causal else (lambda i, j: j)
    in_specs = [pl.BlockSpec((1, tq, d), lambda bb, h, i, j: (bb, i, h)),
                pl.BlockSpec((1, tk, d), lambda bb, h, i, j: (bb, kj(i, j), h)),
                pl.BlockSpec((1, tk, d), lambda bb, h, i, j: (bb, kj(i, j), h))]
    args = [q, k, v]
    if bias is not None:
        hb, nb = bias.shape[:2]
        hsel = (lambda h: h) if hb > 1 else (lambda h: 0)
        slot = (lambda i, j: jnp.clip(i - j, 0, nb - 1)) if causal else (lambda i, j: 0)
        in_specs.append(pl.BlockSpec((1, 1, tq, tk), lambda bb, h, i, j: (hsel(h), slot(i, j), 0, 0)))
        args.append(bias)
    if decay is not None:
        in_specs.append(pl.BlockSpec((1, 1, tq, 1), lambda bb, h, i, j: (bb, h, i, 0)))
        in_specs.append(pl.BlockSpec((1, 1, 1, tk), lambda bb, h, i, j: (bb, h, 0, kj(i, j))))
        args.extend(decay)
    return pl.pallas_call(
        functools.partial(_flash_kernel, scale=scale, causal=causal, nk=nk,
                          has_bias=bias is not None, has_decay=decay is not None),
        out_shape=jax.ShapeDtypeStruct((b, t_q, n_heads * d), BF16),
        grid=(b, n_heads, nq, nk),
        in_specs=in_specs,
        out_specs=pl.BlockSpec((1, tq, d), lambda bb, h, i, j: (bb, i, h)),
        scratch_shapes=[pltpu.VMEM((tq, 1), F32), pltpu.VMEM((tq, 1), F32), pltpu.VMEM((tq, d), F32)],
        compiler_params=_cparams(("parallel", "parallel", "parallel", "arbitrary")),
        name="flash_attention",
    )(*args)


def _diff_kernel(lam_ref, q_ref, k_ref, v_ref, bias_ref, g_ref, o_ref, m_sc, l_sc, acc_sc, *, causal, nk, out_gain):
    i, j = pl.program_id(2), pl.program_id(3)
    last = i if causal else nk - 1

    @pl.when(j == 0)
    def _():
        _softmax_init(m_sc, l_sc, acc_sc)

    @pl.when(j <= last)
    def _():
        q, k, v = q_ref[0], k_ref[0], v_ref[0]
        bias = bias_ref[0, 0]
        for c in range(2):
            s = _qk(q[:, c * DC:(c + 1) * DC], k[:, c * DC:(c + 1) * DC]) * DC ** -0.5 + bias
            _softmax_step(s, v, m_sc.at[c], l_sc.at[c], acc_sc.at[c])

    @pl.when(j == last)
    def _():
        o = acc_sc[0] / l_sc[0] - lam_ref[0, 0] * (acc_sc[1] / l_sc[1])
        o = o * lax.rsqrt(jnp.mean(o * o, axis=-1, keepdims=True) + LN_EPS) * g_ref[...] * out_gain
        o_ref[0] = o.astype(o_ref.dtype)


def _diff_call(q, k, v, bias, lam, g_sub, out_gain, tq, tk, causal):
    b, t_q, _ = q.shape
    t_k = k.shape[1]
    nq, nk = t_q // tq, t_k // tk
    d = HEAD_DIM
    nb = bias.shape[1]
    kj = (lambda i, j: jnp.minimum(j, i)) if causal else (lambda i, j: j)
    slot = (lambda i, j: jnp.clip(i - j, 0, nb - 1)) if causal else (lambda i, j: 0)
    return pl.pallas_call(
        functools.partial(_diff_kernel, causal=causal, nk=nk, out_gain=out_gain),
        out_shape=jax.ShapeDtypeStruct((b, t_q, H_C * d), BF16),
        grid=(b, H_C, nq, nk),
        in_specs=[pl.BlockSpec(memory_space=pltpu.SMEM),
                  pl.BlockSpec((1, tq, d), lambda bb, h, i, j: (bb, i, h)),
                  pl.BlockSpec((1, tk, d), lambda bb, h, i, j: (bb, kj(i, j), h)),
                  pl.BlockSpec((1, tk, d), lambda bb, h, i, j: (bb, kj(i, j), h)),
                  pl.BlockSpec((1, 1, tq, tk), lambda bb, h, i, j: (h, slot(i, j), 0, 0)),
                  pl.BlockSpec((1, d), lambda bb, h, i, j: (0, 0))],
        out_specs=pl.BlockSpec((1, tq, d), lambda bb, h, i, j: (bb, i, h)),
        scratch_shapes=[pltpu.VMEM((2, tq, 1), F32), pltpu.VMEM((2, tq, 1), F32), pltpu.VMEM((2, tq, d), F32)],
        compiler_params=_cparams(("parallel", "parallel", "parallel", "arbitrary")),
        name="diff_attention",
    )(lam.reshape(1, 1).astype(F32), q, k, v, bias, g_sub.reshape(1, d).astype(F32))


def _sort_key(x):
    bits = pltpu.bitcast(x, jnp.int32)
    return bits ^ ((bits >> 31) & jnp.int32(0x7FFFFFFF))


def _dsa_score_kernel(iq_ref, iw_ref, ik_ref, s_ref, thr_ref, key_sc, *, q_off, t_real, topk, tkc):
    i = pl.program_id(1)
    tq, t_k = s_ref.shape[1], s_ref.shape[2]
    nck = t_k // tkc
    qpos = q_off + i * tq + lax.broadcasted_iota(jnp.int32, (tq, tkc), 0)
    col_end = jnp.minimum((((q_off + (i + 1) * tq - 1) >> CHUNK_SHIFT) + 1) * CHUNK, t_real)
    iq = iq_ref[0]
    w = iw_ref[0] * IDX_SCALE
    for c in range(nck):
        @pl.when(c * tkc < col_end)
        def _():
            ik = ik_ref[0, c * tkc:(c + 1) * tkc, :]
            acc = jnp.zeros((tq, tkc), F32)
            for h in range(H_IDX):
                sc = _qk(iq[:, h * D_IDX:(h + 1) * D_IDX], ik)
                acc = acc + w[:, h:h + 1] * jnp.maximum(sc, 0.0)
            kpos = c * tkc + lax.broadcasted_iota(jnp.int32, (tq, tkc), 1)
            adm = ((kpos >> CHUNK_SHIFT) <= (qpos >> CHUNK_SHIFT)) & (kpos < t_real)
            sc = jnp.where(adm, acc + 0.0, -jnp.inf)
            s_ref[0, :, c * tkc:(c + 1) * tkc] = sc
            key_sc[:, c * tkc:(c + 1) * tkc] = _sort_key(sc)

        @pl.when(c * tkc >= col_end)
        def _():
            s_ref[0, :, c * tkc:(c + 1) * tkc] = jnp.full((tq, tkc), -jnp.inf, F32)
            key_sc[:, c * tkc:(c + 1) * tkc] = jnp.full((tq, tkc), _NEG_INF_KEY, jnp.int32)

    int_min = jnp.int32(-2 ** 31)
    ans = jnp.full((tq, 1), int_min, jnp.int32)
    for bit in range(31, -1, -1):
        cand = (ans ^ int_min) if bit == 31 else (ans | jnp.int32(1 << bit))
        cnt = jnp.sum(jnp.where(key_sc[...] >= cand, 1.0, 0.0), axis=-1, keepdims=True)
        ans = jnp.where(cnt >= float(topk), cand, ans)
    bits = ans ^ ((ans >> 31) & jnp.int32(0x7FFFFFFF))
    thr_ref[0] = pltpu.bitcast(bits, F32)


_NEG_INF_KEY = int(np.int32(np.array(-np.inf, np.float32).view(np.int32)) ^ np.int32(0x7FFFFFFF))


def _dsa_score_call(iq, iw, ik, q_off, t_real, topk, tq):
    b, t_q, _ = iq.shape
    t_k = ik.shape[1]
    tkc = _pick(t_k, (SCORE_COLS, 128))
    return pl.pallas_call(
        functools.partial(_dsa_score_kernel, q_off=q_off, t_real=t_real, topk=topk, tkc=tkc),
        out_shape=(jax.ShapeDtypeStruct((b, t_q, t_k), F32), jax.ShapeDtypeStruct((b, t_q, 1), F32)),
        grid=(b, t_q // tq),
        in_specs=[pl.BlockSpec((1, tq, H_IDX * D_IDX), lambda bb, i: (bb, i, 0)),
                  pl.BlockSpec((1, tq, H_IDX), lambda bb, i: (bb, i, 0)),
                  pl.BlockSpec((1, t_k, D_IDX), lambda bb, i: (bb, 0, 0))],
        out_specs=(pl.BlockSpec((1, tq, t_k), lambda bb, i: (bb, i, 0)),
                   pl.BlockSpec((1, tq, 1), lambda bb, i: (bb, i, 0))),
        scratch_shapes=[pltpu.VMEM((tq, t_k), jnp.int32)],
        compiler_params=_cparams(("parallel", "parallel")),
        name="dsa_indexer",
    )(iq, iw, ik)


def _dsa_attn_kernel(q_ref, k_ref, v_ref, s_ref, thr_ref, bias_ref, o_ref, m_sc, l_sc, acc_sc, *, causal, nk):
    i, j = pl.program_id(1), pl.program_id(2)
    last = i if causal else nk - 1
    d = HEAD_DIM

    @pl.when(j == 0)
    def _():
        _softmax_init(m_sc, l_sc, acc_sc)

    @pl.when(j <= last)
    def _():
        keep = s_ref[0] >= thr_ref[0]
        for h in range(H_B):
            s = _qk(q_ref[0, :, h * d:(h + 1) * d], k_ref[0, :, h * d:(h + 1) * d]) * d ** -0.5 + bias_ref[h, 0]
            s = jnp.where(keep, s, NEG)
            _softmax_step(s, v_ref[0, :, h * d:(h + 1) * d], m_sc.at[h], l_sc.at[h], acc_sc.at[h])

    @pl.when(j == last)
    def _():
        for h in range(H_B):
            o_ref[0, :, h * d:(h + 1) * d] = (acc_sc[h] / l_sc[h]).astype(o_ref.dtype)


def _dsa_attn_call(q, k, v, score, thr, bias, tq, tk, causal):
    b, t_q, _ = q.shape
    t_k = k.shape[1]
    nq, nk = t_q // tq, t_k // tk
    w = H_B * HEAD_DIM
    nb = bias.shape[1]
    kj = (lambda i, j: jnp.minimum(j, i)) if causal else (lambda i, j: j)
    slot = (lambda i, j: jnp.clip(i - j, 0, nb - 1)) if causal else (lambda i, j: 0)
    return pl.pallas_call(
        functools.partial(_dsa_attn_kernel, causal=causal, nk=nk),
        out_shape=jax.ShapeDtypeStruct((b, t_q, w), BF16),
        grid=(b, nq, nk),
        in_specs=[pl.BlockSpec((1, tq, w), lambda bb, i, j: (bb, i, 0)),
                  pl.BlockSpec((1, tk, w), lambda bb, i, j: (bb, kj(i, j), 0)),
                  pl.BlockSpec((1, tk, w), lambda bb, i, j: (bb, kj(i, j), 0)),
                  pl.BlockSpec((1, tq, tk), lambda bb, i, j: (bb, i, kj(i, j))),
                  pl.BlockSpec((1, tq, 1), lambda bb, i, j: (bb, i, 0)),
                  pl.BlockSpec((H_B, 1, tq, tk), lambda bb, i, j: (0, slot(i, j), 0, 0))],
        out_specs=pl.BlockSpec((1, tq, w), lambda bb, i, j: (bb, i, 0)),
        scratch_shapes=[pltpu.VMEM((H_B, tq, 1), F32), pltpu.VMEM((H_B, tq, 1), F32),
                        pltpu.VMEM((H_B, tq, HEAD_DIM), F32)],
        compiler_params=_cparams(("parallel", "parallel", "arbitrary")),
        name="dsa_attention",
    )(q, k, v, score, thr, bias)


def _top16_rows(s, payload=None):
    r = s.shape[0]
    iota = lax.broadcasted_iota(jnp.int32, s.shape, 0).astype(F32)
    vals, picks = [], []
    for _ in range(PEER_TOPK):
        m = jnp.max(s, axis=0, keepdims=True)
        am = jnp.min(jnp.where(s == m, iota, float(r)), axis=0, keepdims=True)
        hit = iota == am
        vals.append(m)
        if payload is None:
            picks.append(am)
        else:
            picks.append(jnp.sum(jnp.where(hit, payload, 0.0), axis=0, keepdims=True))
        s = jnp.where(hit, -jnp.inf, s)
    return jnp.concatenate(vals, axis=0), jnp.concatenate(picks, axis=0)


def _peer_select_kernel(q_ref, keys_ref, idx_ref, gate_ref):
    for h in range(PEER_HEADS):
        sv, si = [], []
        for c in range(2):
            hc = 2 * h + c
            s = _qk(keys_ref[hc], q_ref[:, hc * 128:(hc + 1) * 128])
            v, ix = _top16_rows(s)
            sv.append(v)
            si.append(ix)
        cand = jnp.concatenate([sv[0][a:a + 1] + sv[1] for a in range(PEER_TOPK)], axis=0)
        cidx = jnp.concatenate([si[0][a:a + 1] * float(N_KEYS) + si[1] for a in range(PEER_TOPK)], axis=0)
        fv, eidx = _top16_rows(cand, payload=cidx)
        e = jnp.exp(fv - fv[0:1])
        gate = e / jnp.sum(e, axis=0, keepdims=True)
        idx_ref[0, h * PEER_TOPK:(h + 1) * PEER_TOPK, :] = eidx.astype(jnp.int32)
        gate_ref[0, h * PEER_TOPK:(h + 1) * PEER_TOPK, :] = gate


def _peer_select_call(q, keys):
    n, dq = q.shape
    t = _pick(n, (PEER_SEL_TOKENS, 128))
    out = pl.BlockSpec((1, N_PICK, t), lambda i: (i, 0, 0))
    return pl.pallas_call(
        _peer_select_kernel,
        out_shape=(jax.ShapeDtypeStruct((n // t, N_PICK, t), jnp.int32),
                   jax.ShapeDtypeStruct((n // t, N_PICK, t), F32)),
        grid=(n // t,),
        in_specs=[pl.BlockSpec((t, dq), lambda i: (i, 0)),
                  pl.BlockSpec(keys.shape, lambda i: (0, 0, 0))],
        out_specs=(out, out),
        compiler_params=_cparams(("parallel",)),
        name="peer_select",
    )(q, keys)


def _row_copy(tab_ref, buf_ref, sem, slot, e, r):
    return pltpu.make_async_copy(tab_ref.at[pl.ds(e, 1), :], buf_ref.at[slot, pl.ds(r, 1), :], sem.at[slot])


def _peer_group(buf_ref, slot, x_row, gate_col, r0):
    words = buf_ref[slot, r0:r0 + 8, :]
    u = pltpu.bitcast(words & jnp.uint32(0xFFFF0000), F32)
    v = pltpu.bitcast(words << 16, F32)
    h = jnp.sum(u * x_row, axis=-1, keepdims=True)
    act = 0.5 * h * (1.0 + lax.erf(h * (2.0 ** -0.5)))
    return (gate_col * act) * v


def _peer_expert_kernel(idx_ref, x_ref, gate_ref, tab_ref, o_ref, buf, sem, *, nt):
    i = pl.program_id(0)
    rows = PEER_TILE * N_PICK
    issue_slot = lax.rem(i, 2)
    work_slot = 1 - issue_slot

    def issue(r):
        _row_copy(tab_ref, buf, sem, issue_slot, idx_ref[0, 0, r], r).start()

    def compute(interleave_issue):
        pltpu.make_async_copy(tab_ref.at[pl.ds(0, rows), :], buf.at[work_slot], sem.at[work_slot]).wait()
        for t in range(PEER_TILE):
            x_row = x_ref[t:t + 1, :]
            acc = None
            for g in range(N_PICK // 8):
                r0 = t * N_PICK + g * 8
                if interleave_issue:
                    for r in range(r0, r0 + 8):
                        issue(r)
                c = _peer_group(buf, work_slot, x_row, gate_ref[0, g * 8:(g + 1) * 8, t:t + 1], r0)
                acc = c if acc is None else acc + c
            o_ref[t:t + 1, :] = jnp.sum(acc, axis=0, keepdims=True)

    @pl.when(i == 0)
    def _():
        def body(r, carry):
            issue(r)
            return carry
        lax.fori_loop(0, rows, body, 0)

    @pl.when((i > 0) & (i < nt))
    def _():
        compute(True)

    @pl.when(i == nt)
    def _():
        compute(False)


def _peer_expert_call(idx, x, gate, table):
    n, d = x.shape
    nt = n // PEER_TILE
    rows = PEER_TILE * N_PICK
    prev = lambda i: jnp.maximum(i - 1, 0)
    return pl.pallas_call(
        functools.partial(_peer_expert_kernel, nt=nt),
        out_shape=jax.ShapeDtypeStruct((n, d), F32),
        grid=(nt + 1,),
        in_specs=[pl.BlockSpec((1, 1, rows), lambda i: (jnp.minimum(i, nt - 1), 0, 0), memory_space=pltpu.SMEM),
                  pl.BlockSpec((PEER_TILE, d), lambda i: (prev(i), 0)),
                  pl.BlockSpec((1, N_PICK, PEER_TILE), lambda i: (prev(i), 0, 0)),
                  pl.BlockSpec(memory_space=pl.ANY)],
        out_specs=pl.BlockSpec((PEER_TILE, d), lambda i: (prev(i), 0)),
        scratch_shapes=[pltpu.VMEM((2, rows, d), jnp.uint32), pltpu.SemaphoreType.DMA((2,))],
        compiler_params=_cparams(("arbitrary",)),
        name="peer_experts",
    )(idx, x, gate, table)


def _rel_bucket(rel):
    half = N_BUCKETS // 2
    exact = half // 2
    base = jnp.where(rel > 0, half, 0)
    n = jnp.abs(rel)
    nf = jnp.maximum(n, 1).astype(F32)
    large = exact + (jnp.log(nf / exact) / math.log(MAX_DISTANCE / exact) * (half - exact)).astype(jnp.int32)
    large = jnp.minimum(large, half - 1)
    return base + jnp.where(n < exact, n, large)


def _bucket_saturation():
    half = N_BUCKETS // 2
    exact = half // 2
    n = np.arange(1, 1 << 16, dtype=np.float32)
    large = np.minimum(exact + (np.log(n / exact) / math.log(MAX_DISTANCE / exact) * (half - exact)).astype(np.int32),
                       half - 1)
    return int(np.max(np.nonzero(large < half - 1)[0]) + 2)


def _bias_blocks(table, t, mask):
    assert t % CHUNK == 0 and t >= _bucket_saturation()
    qi = jnp.arange(t)[:, None]
    kj = jnp.arange(t)[None, :]
    ok = (kj // CHUNK <= qi // CHUNK) if mask == "chunk" else (kj <= qi)
    masks = [jnp.where(ok, 0.0, NEG).astype(F32), jnp.zeros((t, t), F32), jnp.zeros((t, t), F32)]
    if table is None:
        return jnp.stack(masks[:2])[None]
    blocks = [jnp.moveaxis(table[_rel_bucket(kj - qi - delta * t)], -1, 0).astype(F32) + masks[delta][None]
              for delta in range(3)]
    return jnp.stack(blocks, axis=1)


def _bias_rows(table, qpos, t_k, t_real, mask):
    kpos = jnp.arange(t_k)[None, :]
    qp = qpos[:, None]
    ok = (kpos // CHUNK <= qp // CHUNK) if mask == "chunk" else (kpos <= qp)
    m = jnp.where(ok & (kpos < t_real), 0.0, NEG).astype(F32)
    if table is None:
        return m[None, None]
    return (jnp.moveaxis(table[_rel_bucket(kpos - qp)], -1, 0).astype(F32) + m[None])[:, None]


def _in_proj_weights(w_in):
    a, b, c = H_A * HEAD_DIM, H_B * HEAD_DIM, H_C * HEAD_DIM
    sizes = (a, a, a, H_A, b, b, b, H_IDX * D_IDX, D_IDX, H_IDX, c, c, c)
    offs = np.concatenate([[0], np.cumsum(sizes)])
    col = lambda i: w_in[:, offs[i]:offs[i + 1]]
    qa, ka, va, fa, qb, kb, vb, iq, ik, iw, qc, kc, vc = [col(i) for i in range(13)]
    main = jnp.concatenate([qa, ka, va, qb, kb, vb, iq, qc, kc, vc], axis=1)
    tail = jnp.concatenate([ik, iw, fa], axis=1)
    tail = jnp.pad(tail, ((0, 0), (0, 128 - tail.shape[1])))
    return main.astype(BF16), tail.astype(BF16)


def _pack_tables(u_tab, v_tab):
    hi = lax.bitcast_convert_type(u_tab.astype(BF16), jnp.uint16).astype(jnp.uint32)
    lo = lax.bitcast_convert_type(v_tab.astype(BF16), jnp.uint16).astype(jnp.uint32)
    return (hi << 16) | lo


def _project(xb, w_main, w_tail, b_f):
    b, t, d = xb.shape
    x2 = xb.reshape(b * t, d)
    main = _matmul(x2, w_main)
    tail = _matmul(x2, w_tail)
    a, bb, c = H_A * HEAD_DIM, H_B * HEAD_DIM, H_C * HEAD_DIM
    sizes = (a, a, a, bb, bb, bb, H_IDX * D_IDX, c, c, c)
    offs = np.concatenate([[0], np.cumsum(sizes)])
    names = ("qa", "ka", "va", "qb", "kb", "vb", "iq", "qc", "kc", "vc")
    out = {nm: main[:, offs[i]:offs[i + 1]].reshape(b, t, -1) for i, nm in enumerate(names)}
    out["ik"] = tail[:, :D_IDX].reshape(b, t, D_IDX)
    out["iw"] = tail[:, D_IDX:D_IDX + H_IDX].reshape(b, t, H_IDX)
    fa = tail[:, D_IDX + H_IDX:D_IDX + H_IDX + H_A].reshape(b, t, H_A)
    out["logf"] = jax.nn.log_sigmoid(fa + b_f.astype(F32))
    return out


def _rows_out(p):
    b, t = p["ka"].shape[:2]
    hd = lambda a, n: a.reshape(b, t, n, HEAD_DIM)
    return (hd(p["ka"], H_A), hd(p["va"], H_A), p["logf"], hd(p["kb"], H_B), hd(p["vb"], H_B), p["ik"],
            hd(p["kc"], H_C), hd(p["vc"], H_C))


def _mixer_prompt(xb, lw, lam, lam_init, rel_bias):
    b, t, _ = xb.shape
    p = _project(xb, lw["w_main"], lw["w_tail"], lw["b_f"])
    blk = min(ATT_BLOCK, t)
    topk = min(IDX_TOPK_MAX, t // 4)
    bf = lambda nm: p[nm].astype(BF16)
    dcum = jnp.cumsum(p["logf"], axis=1)
    dq = jnp.moveaxis(dcum, 2, 1)[..., None]
    dk = jnp.moveaxis(dcum, 2, 1)[:, :, None, :]
    oa = _flash_call(bf("qa"), bf("ka"), bf("va"), H_A, blk, blk, HEAD_DIM ** -0.5, True,
                     bias=_bias_blocks(None, blk, "causal"), decay=(dq, dk))
    score, thr = _dsa_score_call(bf("iq"), p["iw"], bf("ik"), 0, t, topk, min(SCORE_ROWS, t))
    ob = _dsa_attn_call(bf("qb"), bf("kb"), bf("vb"), score, thr,
                        _bias_blocks(rel_bias[:, :H_B], blk, "chunk"), blk, blk, True)
    oc = _diff_call(bf("qc"), bf("kc"), bf("vc"), _bias_blocks(rel_bias[:, H_B:], blk, "chunk"),
                    lam, lw["g_sub"], 1.0 - lam_init, blk, blk, True)
    o = jnp.concatenate([oa, ob, oc], axis=-1).reshape(b * t, -1)
    return _matmul(o, lw["w_out"]), _rows_out(p)


def _mixer_sample(xb, caches, lw, lam, lam_init, rel_bias):
    ca_k, ca_v, ca_logf, cb_k, cb_v, cb_ik, cc_k, cc_v = caches
    b, t, _ = xb.shape
    p_len = ca_k.shape[1]
    t_real = p_len + t
    t_k = -(-t_real // 128) * 128
    p = _project(xb, lw["w_main"], lw["w_tail"], lw["b_f"])
    bf = lambda nm: p[nm].astype(BF16)

    def cat(cache, new):
        full = jnp.concatenate([cache.reshape(b, p_len, -1).astype(BF16), new.astype(BF16)], axis=1)
        return jnp.pad(full, ((0, 0), (0, t_k - t_real), (0, 0)))

    dcum = jnp.cumsum(jnp.concatenate([ca_logf.astype(F32), p["logf"]], axis=1), axis=1)
    dq = jnp.moveaxis(dcum[:, p_len:], 2, 1)[..., None]
    dk = jnp.pad(jnp.moveaxis(dcum, 2, 1), ((0, 0), (0, 0), (0, t_k - t_real)))[:, :, None, :]
    qpos = p_len + jnp.arange(t)
    topk = min(IDX_TOPK_MAX, t_real // 4)
    oa = _flash_call(bf("qa"), cat(ca_k, p["ka"]), cat(ca_v, p["va"]), H_A, t, t_k, HEAD_DIM ** -0.5, False,
                     bias=_bias_rows(None, qpos, t_k, t_real, "causal"), decay=(dq, dk))
    score, thr = _dsa_score_call(bf("iq"), p["iw"], cat(cb_ik, p["ik"]), p_len, t_real, topk, t)
    ob = _dsa_attn_call(bf("qb"), cat(cb_k, p["kb"]), cat(cb_v, p["vb"]), score, thr,
                        _bias_rows(rel_bias[:, :H_B], qpos, t_k, t_real, "chunk"), t, t_k, False)
    oc = _diff_call(bf("qc"), cat(cc_k, p["kc"]), cat(cc_v, p["vc"]),
                    _bias_rows(rel_bias[:, H_B:], qpos, t_k, t_real, "chunk"),
                    lam, lw["g_sub"], 1.0 - lam_init, t, t_k, False)
    o = jnp.concatenate([oa, ob, oc], axis=-1).reshape(b * t, -1)
    return _matmul(o, lw["w_out"]), _rows_out(p)


def _mem_attend(xb, mk, mv, lw):
    b, t, d = xb.shape
    q = _matmul(xb.reshape(b * t, d), lw["w_mq"]).astype(BF16).reshape(b, t, -1)
    tq = _pick(t, (ATT_BLOCK, 256, 128))
    o = _flash_call(q, mk, mv, H_MEM, tq, mk.shape[1], HEAD_DIM ** -0.5, False)
    return _matmul(o.reshape(b * t, -1), lw["w_mo"])


def _peer(x32, xb, lw):
    n, d = x32.shape
    q = _matmul(xb, lw["w_pq"]).astype(BF16)
    idx, gate = _peer_select_call(q, lw["sub_keys"])
    nsel, _, t = idx.shape
    idx = jnp.transpose(idx, (0, 2, 1)).reshape(n // PEER_TILE, 1, PEER_TILE * N_PICK)
    gate = gate.reshape(nsel, N_PICK, t // PEER_TILE, PEER_TILE)
    gate = jnp.transpose(gate, (0, 2, 1, 3)).reshape(n // PEER_TILE, N_PICK, PEER_TILE)
    return _peer_expert_call(idx, x32, gate, lw["table"])


def _lambda_value(lam_p, layer_idx):
    lam_init = 0.8 - 0.6 * math.exp(-0.3 * layer_idx)
    lp = lam_p.astype(F32)
    lam = jnp.exp(jnp.sum(lp[0] * lp[1])) - jnp.exp(jnp.sum(lp[2] * lp[3])) + lam_init
    return lam, lam_init


def kernel(x_prompt, x_sample, cache_a_k, cache_a_v, cache_a_logf, cache_b_k, cache_b_v, cache_b_ik, cache_c_k,
           cache_c_v, cache_mem_k, cache_mem_v, mem_prompt, ln_in_g, ln_in_b, w_in, b_f, w_out, diff_lambda,
           diff_subln, rel_bias, ln1_g, ln1_b, w_mq, w_mk, w_mv, w_mo, ln2_g, ln2_b, w_pq, sub_keys, u_tab,
           v_tab, ln3_g, ln3_b):
    depth = w_in.shape[0]
    alpha = (2 * depth) ** 0.25
    bp, tp, d = x_prompt.shape
    bs, ts, _ = x_sample.shape
    n_mem = mem_prompt.shape[1]

    xp32, xp16 = _ln_call(x_prompt.reshape(bp * tp, d), None, ln_in_g, ln_in_b, 1.0)
    xs32, xs16 = _ln_call(x_sample.reshape(bs * ts, d), None, ln_in_g, ln_in_b, 1.0)
    mem16 = mem_prompt.reshape(bp * n_mem, d).astype(BF16)

    p_rows = [[] for _ in range(8)]
    s_rows = [[] for _ in range(8)]
    p_mk, p_mv = [], []
    for l in range(depth):
        lam, lam_init = _lambda_value(diff_lambda[l], l)
        w_main, w_tail = _in_proj_weights(w_in[l])
        lw = dict(w_main=w_main, w_tail=w_tail, b_f=b_f[l], w_out=w_out[l].astype(BF16), g_sub=diff_subln[l],
                  w_mq=w_mq[l].astype(BF16), w_mo=w_mo[l].astype(BF16), w_pq=w_pq[l].astype(BF16),
                  sub_keys=sub_keys[l].reshape(2 * PEER_HEADS, N_KEYS, -1).astype(BF16),
                  table=_pack_tables(u_tab[l], v_tab[l]))

        y, rows = _mixer_prompt(xp16.reshape(bp, tp, d), lw, lam, lam_init, rel_bias)
        xp32, xp16 = _ln_call(xp32, y, ln1_g[l], ln1_b[l], alpha)
        mk = _matmul(mem16, w_mk[l].astype(BF16))
        mv = _matmul(mem16, w_mv[l].astype(BF16))
        y = _mem_attend(xp16.reshape(bp, tp, d), mk.astype(BF16).reshape(bp, n_mem, -1),
                        mv.astype(BF16).reshape(bp, n_mem, -1), lw)
        xp32, xp16 = _ln_call(xp32, y, ln2_g[l], ln2_b[l], alpha)
        y = _peer(xp32, xp16, lw)
        xp32, xp16 = _ln_call(xp32, y, ln3_g[l], ln3_b[l], alpha)
        for i in range(8):
            p_rows[i].append(rows[i])
        p_mk.append(mk.reshape(bp, n_mem, H_MEM, HEAD_DIM))
        p_mv.append(mv.reshape(bp, n_mem, H_MEM, HEAD_DIM))

        caches = (cache_a_k[l], cache_a_v[l], cache_a_logf[l], cache_b_k[l], cache_b_v[l], cache_b_ik[l],
                  cache_c_k[l], cache_c_v[l])
        y, rows = _mixer_sample(xs16.reshape(bs, ts, d), caches, lw, lam, lam_init, rel_bias)
        xs32, xs16 = _ln_call(xs32, y, ln1_g[l], ln1_b[l], alpha)
        y = _mem_attend(xs16.reshape(bs, ts, d), cache_mem_k[l].reshape(bs, n_mem, -1).astype(BF16),
                        cache_mem_v[l].reshape(bs, n_mem, -1).astype(BF16), lw)
        xs32, xs16 = _ln_call(xs32, y, ln2_g[l], ln2_b[l], alpha)
        y = _peer(xs32, xs16, lw)
        xs32, xs16 = _ln_call(xs32, y, ln3_g[l], ln3_b[l], alpha)
        for i in range(8):
            s_rows[i].append(rows[i])

    ps = [jnp.stack(r) for r in p_rows]
    ss = [jnp.stack(r) for r in s_rows]
    return (xp32.reshape(bp, tp, d), xs32.reshape(bs, ts, d), ps[0], ps[1], ps[2], ps[3], ps[4], ps[5], ps[6],
            ps[7], jnp.stack(p_mk), jnp.stack(p_mv), ss[0], ss[1], ss[2], ss[3], ss[4], ss[5], ss[6], ss[7])
```

```python
import functools
import math

import jax
import jax.numpy as jnp
import numpy as np
from jax import lax
from jax.experimental import pallas as pl
from jax.experimental.pallas import tpu as pltpu

F32 = jnp.float32
BF16 = jnp.bfloat16

HEAD_DIM = 128
CHUNK = 64
CHUNK_SHIFT = 6
H_A, H_B, H_C = 6, 6, 4
DC = HEAD_DIM // 2
H_IDX, D_IDX = 16, 64
IDX_SCALE = (H_IDX * D_IDX) ** -0.5
IDX_TOPK_MAX = 256
N_BUCKETS = 32
MAX_DISTANCE = 256
H_MEM = 4
PEER_HEADS = 8
N_KEYS = 128
PEER_TOPK = 16
LN_EPS = 1e-5
NEG = -1e30

VMEM_LIMIT = 56 * 1024 * 1024
ATT_BLOCK = 512
FOX_BLOCK = 1024
DIFF_BLOCK = 1024
SCORE_ROWS = 128
SCORE_COLS = 512
PEER_SEL_TOKENS = 256
PEER_TILE = 8
N_PICK = PEER_HEADS * PEER_TOPK


def _cparams(sem):
    return pltpu.CompilerParams(dimension_semantics=sem, vmem_limit_bytes=VMEM_LIMIT)


def _ln_kernel(*refs, alpha, has_y):
    if has_y:
        x_ref, y_ref, g_ref, b_ref, o32_ref, o16_ref = refs
        z = alpha * x_ref[...] + y_ref[...]
    else:
        x_ref, g_ref, b_ref, o32_ref, o16_ref = refs
        z = x_ref[...]
    mu = jnp.mean(z, axis=-1, keepdims=True)
    zc = z - mu
    var = jnp.mean(zc * zc, axis=-1, keepdims=True)
    o = zc * lax.rsqrt(var + LN_EPS) * g_ref[...] + b_ref[...]
    o32_ref[...] = o
    o16_ref[...] = o.astype(BF16)


def _ln_call(x, y, g, b, alpha):
    n, d = x.shape
    tm = 256 if n % 256 == 0 else n
    row = pl.BlockSpec((tm, d), lambda i: (i, 0))
    vec = pl.BlockSpec((1, d), lambda i: (0, 0))
    has_y = y is not None
    args = (x, y) if has_y else (x,)
    return pl.pallas_call(
        functools.partial(_ln_kernel, alpha=alpha, has_y=has_y),
        out_shape=(jax.ShapeDtypeStruct((n, d), F32), jax.ShapeDtypeStruct((n, d), BF16)),
        grid=(n // tm,),
        in_specs=[row] * len(args) + [vec, vec],
        out_specs=(row, row),
        compiler_params=_cparams(("parallel",)),
        name="layer_norm",
    )(*args, g.reshape(1, d), b.reshape(1, d))


def _matmul_kernel(x_ref, w_ref, o_ref):
    o_ref[...] = jnp.dot(x_ref[...], w_ref[...], preferred_element_type=F32)


def _pick(n, cands):
    for c in cands:
        if n % c == 0:
            return c
    return n


def _matmul(x, w):
    m, k = x.shape
    n = w.shape[1]
    tm = _pick(m, (512, 256, 128))
    tn = _pick(n, (1024, 512, 256, 128))
    return pl.pallas_call(
        _matmul_kernel,
        out_shape=jax.ShapeDtypeStruct((m, n), F32),
        grid=(n // tn, m // tm),
        in_specs=[pl.BlockSpec((tm, k), lambda j, i: (i, 0)),
                  pl.BlockSpec((k, tn), lambda j, i: (0, j))],
        out_specs=pl.BlockSpec((tm, tn), lambda j, i: (i, j)),
        compiler_params=_cparams(("parallel", "parallel")),
        name="projection",
    )(x, w)


def _softmax_init(m_sc, l_sc, acc_sc):
    m_sc[...] = jnp.full(m_sc.shape, NEG, F32)
    l_sc[...] = jnp.zeros(l_sc.shape, F32)
    acc_sc[...] = jnp.zeros(acc_sc.shape, F32)


def _softmax_step(s, v, m_ref, l_ref, acc_ref):
    m_prev = m_ref[...]
    m_new = jnp.maximum(m_prev, jnp.max(s, axis=-1, keepdims=True))
    a = jnp.exp(m_prev - m_new)
    p = jnp.exp(s - m_new)
    l_ref[...] = a * l_ref[...] + jnp.sum(p, axis=-1, keepdims=True)
    acc_ref[...] = a * acc_ref[...] + jnp.dot(p.astype(BF16), v, preferred_element_type=F32)
    m_ref[...] = m_new


def _qk(q, k):
    return lax.dot_general(q, k, (((1,), (1,)), ((), ())), preferred_element_type=F32)


def _flash_kernel(*refs, scale, causal, nk, has_bias, has_decay):
    refs = list(refs)
    q_ref, k_ref, v_ref = refs[:3]
    pos = 3
    bias_ref = dk_ref = None
    if has_bias:
        bias_ref = refs[pos]
        pos += 1
    if has_decay:
        dk_ref = refs[pos]
        pos += 1
    o_ref, m_sc, l_sc, acc_sc = refs[pos:pos + 4]
    i, j = pl.program_id(2), pl.program_id(3)
    last = i if causal else nk - 1

    @pl.when(j == 0)
    def _():
        _softmax_init(m_sc, l_sc, acc_sc)

    def step(with_bias):
        s = _qk(q_ref[0], k_ref[0]) * scale
        if has_decay:
            s = s - dk_ref[0, 0]
        if with_bias:
            s = s + bias_ref[0, 0]
        _softmax_step(s, v_ref[0], m_sc, l_sc, acc_sc)

    if causal:
        pl.when(j < last)(lambda: step(False))
        pl.when(j == last)(lambda: step(has_bias))
    else:
        step(has_bias)

    @pl.when(j == last)
    def _():
        o_ref[0] = (acc_sc[...] / l_sc[...]).astype(o_ref.dtype)


def _flash_call(q, k, v, n_heads, tq, tk, scale, causal, bias=None, decay=None):
    b, t_q, _ = q.shape
    t_k = k.shape[1]
    nq, nk = t_q // tq, t_k // tk
    d = HEAD_DIM
    kj = (lambda i, j: jnp.minimum(j, i)) if causal else (lambda i, j: j)
    in_specs = [pl.BlockSpec((1, tq, d), lambda bb, h, i, j: (bb, i, h)),
                pl.BlockSpec((1, tk, d), lambda bb, h, i, j: (bb, kj(i, j), h)),
                pl.BlockSpec((1, tk, d), lambda bb, h, i, j: (bb, kj(i, j), h))]
    args = [q, k, v]
    if bias is not None:
        hsel = (lambda h: h) if bias.shape[0] > 1 else (lambda h: 0)
        in_specs.append(pl.BlockSpec((1, 1, tq, tk), lambda bb, h, i, j: (hsel(h), 0, 0, 0)))
        args.append(bias)
    if decay is not None:
        in_specs.append(pl.BlockSpec((1, 1, 1, tk), lambda bb, h, i, j: (bb, h, 0, kj(i, j))))
        args.append(decay)
    return pl.pallas_call(
        functools.partial(_flash_kernel, scale=scale, causal=causal, nk=nk,
                          has_bias=bias is not None, has_decay=decay is not None),
        out_shape=jax.ShapeDtypeStruct((b, t_q, n_heads * d), BF16),
        grid=(b, n_heads, nq, nk),
        in_specs=in_specs,
        out_specs=pl.BlockSpec((1, tq, d), lambda bb, h, i, j: (bb, i, h)),
        scratch_shapes=[pltpu.VMEM((tq, 1), F32), pltpu.VMEM((tq, 1), F32), pltpu.VMEM((tq, d), F32)],
        compiler_params=_cparams(("parallel", "parallel", "parallel", "arbitrary")),
        name="flash_attention",
    )(*args)


def _diff_kernel(lam_ref, q_ref, k_ref, v_ref, bias_ref, g_ref, o_ref, m_sc, l_sc, acc_sc, *, causal, nk, out_gain):
    i, j = pl.program_id(2), pl.program_id(3)
    last = i if causal else nk - 1

    @pl.when(j == 0)
    def _():
        _softmax_init(m_sc, l_sc, acc_sc)

    @pl.when(j <= last)
    def _():
        q, k, v = q_ref[0], k_ref[0], v_ref[0]
        bias = bias_ref[0, 0]
        for c in range(2):
            s = _qk(q[:, c * DC:(c + 1) * DC], k[:, c * DC:(c + 1) * DC]) * DC ** -0.5 + bias
            _softmax_step(s, v, m_sc.at[c], l_sc.at[c], acc_sc.at[c])

    @pl.when(j == last)
    def _():
        o = acc_sc[0] / l_sc[0] - lam_ref[0, 0] * (acc_sc[1] / l_sc[1])
        o = o * lax.rsqrt(jnp.mean(o * o, axis=-1, keepdims=True) + LN_EPS) * g_ref[...] * out_gain
        o_ref[0] = o.astype(o_ref.dtype)


def _diff_call(q, k, v, bias, lam, g_sub, out_gain, tq, tk, causal):
    b, t_q, _ = q.shape
    t_k = k.shape[1]
    nq, nk = t_q // tq, t_k // tk
    d = HEAD_DIM
    nb = bias.shape[1]
    kj = (lambda i, j: jnp.minimum(j, i)) if causal else (lambda i, j: j)
    slot = (lambda i, j: jnp.clip(i - j, 0, nb - 1)) if causal else (lambda i, j: 0)
    return pl.pallas_call(
        functools.partial(_diff_kernel, causal=causal, nk=nk, out_gain=out_gain),
        out_shape=jax.ShapeDtypeStruct((b, t_q, H_C * d), BF16),
        grid=(b, H_C, nq, nk),
        in_specs=[pl.BlockSpec(memory_space=pltpu.SMEM),
                  pl.BlockSpec((1, tq, d), lambda bb, h, i, j: (bb, i, h)),
                  pl.BlockSpec((1, tk, d), lambda bb, h, i, j: (bb, kj(i, j), h)),
                  pl.BlockSpec((1, tk, d), lambda bb, h, i, j: (bb, kj(i, j), h)),
                  pl.BlockSpec((1, 1, tq, tk), lambda bb, h, i, j: (h, slot(i, j), 0, 0)),
                  pl.BlockSpec((1, d), lambda bb, h, i, j: (0, 0))],
        out_specs=pl.BlockSpec((1, tq, d), lambda bb, h, i, j: (bb, i, h)),
        scratch_shapes=[pltpu.VMEM((2, tq, 1), F32), pltpu.VMEM((2, tq, 1), F32), pltpu.VMEM((2, tq, d), F32)],
        compiler_params=_cparams(("parallel", "parallel", "parallel", "arbitrary")),
        name="diff_attention",
    )(lam.reshape(1, 1).astype(F32), q, k, v, bias, g_sub.reshape(1, d).astype(F32))


def _sort_key(x):
    bits = pltpu.bitcast(x, jnp.int32)
    return bits ^ ((bits >> 31) & jnp.int32(0x7FFFFFFF))


def _dsa_score_kernel(iq_ref, iw_ref, ik_ref, s_ref, thr_ref, key_sc, *, q_off, t_real, topk, tkc):
    i = pl.program_id(1)
    tq, t_k = s_ref.shape[1], s_ref.shape[2]
    nck = t_k // tkc
    qpos = q_off + i * tq + lax.broadcasted_iota(jnp.int32, (tq, tkc), 0)
    col_end = jnp.minimum((((q_off + (i + 1) * tq - 1) >> CHUNK_SHIFT) + 1) * CHUNK, t_real)
    iq = iq_ref[0]
    w = iw_ref[0] * IDX_SCALE
    for c in range(nck):
        @pl.when(c * tkc < col_end)
        def _():
            ik = ik_ref[0, c * tkc:(c + 1) * tkc, :]
            acc = jnp.zeros((tq, tkc), F32)
            for h in range(H_IDX):
                sc = _qk(iq[:, h * D_IDX:(h + 1) * D_IDX], ik)
                acc = acc + w[:, h:h + 1] * jnp.maximum(sc, 0.0)
            kpos = c * tkc + lax.broadcasted_iota(jnp.int32, (tq, tkc), 1)
            adm = ((kpos >> CHUNK_SHIFT) <= (qpos >> CHUNK_SHIFT)) & (kpos < t_real)
            sc = jnp.where(adm, acc + 0.0, -jnp.inf)
            s_ref[0, :, c * tkc:(c + 1) * tkc] = sc
            key_sc[c] = _sort_key(sc)

        @pl.when(c * tkc >= col_end)
        def _():
            s_ref[0, :, c * tkc:(c + 1) * tkc] = jnp.full((tq, tkc), -jnp.inf, F32)

    n_need = (col_end + (tkc - 1)) >> (tkc.bit_length() - 1)

    def count_ge(cand):
        def body(c, acc):
            hit = jnp.where(key_sc[c] >= cand, 1.0, 0.0)
            for g in range(tkc // 128):
                acc = acc + hit[:, g * 128:(g + 1) * 128]
            return acc
        acc = lax.fori_loop(0, n_need, body, jnp.zeros((tq, 128), F32))
        return jnp.sum(acc, axis=-1, keepdims=True)

    int_min = jnp.int32(-2 ** 31)
    ans = jnp.full((tq, 1), int_min, jnp.int32)
    for bit in range(31, -1, -1):
        cand = (ans ^ int_min) if bit == 31 else (ans | jnp.int32(1 << bit))
        ans = jnp.where(count_ge(cand) >= float(topk), cand, ans)
    bits = ans ^ ((ans >> 31) & jnp.int32(0x7FFFFFFF))
    thr_ref[0] = jnp.where(ans == int_min, -jnp.inf, pltpu.bitcast(bits, F32))


def _dsa_score_call(iq, iw, ik, q_off, t_real, topk, tq):
    b, t_q, _ = iq.shape
    t_k = ik.shape[1]
    tkc = _pick(t_k, (SCORE_COLS, 128))
    assert tkc & (tkc - 1) == 0 and t_k % tkc == 0
    return pl.pallas_call(
        functools.partial(_dsa_score_kernel, q_off=q_off, t_real=t_real, topk=topk, tkc=tkc),
        out_shape=(jax.ShapeDtypeStruct((b, t_q, t_k), F32), jax.ShapeDtypeStruct((b, t_q, 1), F32)),
        grid=(b, t_q // tq),
        in_specs=[pl.BlockSpec((1, tq, H_IDX * D_IDX), lambda bb, i: (bb, i, 0)),
                  pl.BlockSpec((1, tq, H_IDX), lambda bb, i: (bb, i, 0)),
                  pl.BlockSpec((1, t_k, D_IDX), lambda bb, i: (bb, 0, 0))],
        out_specs=(pl.BlockSpec((1, tq, t_k), lambda bb, i: (bb, i, 0)),
                   pl.BlockSpec((1, tq, 1), lambda bb, i: (bb, i, 0))),
        scratch_shapes=[pltpu.VMEM((t_k // tkc, tq, tkc), jnp.int32)],
        compiler_params=_cparams(("parallel", "parallel")),
        name="dsa_indexer",
    )(iq, iw, ik)


def _dsa_attn_kernel(q_ref, k_ref, v_ref, s_ref, thr_ref, bias_ref, o_ref, m_sc, l_sc, acc_sc, *, causal, nk):
    i, j = pl.program_id(1), pl.program_id(2)
    last = i if causal else nk - 1
    d = HEAD_DIM

    @pl.when(j == 0)
    def _():
        _softmax_init(m_sc, l_sc, acc_sc)

    @pl.when(j <= last)
    def _():
        keep = s_ref[0] >= thr_ref[0]
        for h in range(H_B):
            s = _qk(q_ref[0, :, h * d:(h + 1) * d], k_ref[0, :, h * d:(h + 1) * d]) * d ** -0.5 + bias_ref[h, 0]
            s = jnp.where(keep, s, NEG)
            _softmax_step(s, v_ref[0, :, h * d:(h + 1) * d], m_sc.at[h], l_sc.at[h], acc_sc.at[h])

    @pl.when(j == last)
    def _():
        for h in range(H_B):
            o_ref[0, :, h * d:(h + 1) * d] = (acc_sc[h] / l_sc[h]).astype(o_ref.dtype)


def _dsa_attn_call(q, k, v, score, thr, bias, tq, tk, causal):
    b, t_q, _ = q.shape
    t_k = k.shape[1]
    nq, nk = t_q // tq, t_k // tk
    w = H_B * HEAD_DIM
    nb = bias.shape[1]
    kj = (lambda i, j: jnp.minimum(j, i)) if causal else (lambda i, j: j)
    slot = (lambda i, j: jnp.clip(i - j, 0, nb - 1)) if causal else (lambda i, j: 0)
    return pl.pallas_call(
        functools.partial(_dsa_attn_kernel, causal=causal, nk=nk),
        out_shape=jax.ShapeDtypeStruct((b, t_q, w), BF16),
        grid=(b, nq, nk),
        in_specs=[pl.BlockSpec((1, tq, w), lambda bb, i, j: (bb, i, 0)),
                  pl.BlockSpec((1, tk, w), lambda bb, i, j: (bb, kj(i, j), 0)),
                  pl.BlockSpec((1, tk, w), lambda bb, i, j: (bb, kj(i, j), 0)),
                  pl.BlockSpec((1, tq, tk), lambda bb, i, j: (bb, i, kj(i, j))),
                  pl.BlockSpec((1, tq, 1), lambda bb, i, j: (bb, i, 0)),
                  pl.BlockSpec((H_B, 1, tq, tk), lambda bb, i, j: (0, slot(i, j), 0, 0))],
        out_specs=pl.BlockSpec((1, tq, w), lambda bb, i, j: (bb, i, 0)),
        scratch_shapes=[pltpu.VMEM((H_B, tq, 1), F32), pltpu.VMEM((H_B, tq, 1), F32),
                        pltpu.VMEM((H_B, tq, HEAD_DIM), F32)],
        compiler_params=_cparams(("parallel", "parallel", "arbitrary")),
        name="dsa_attention",
    )(q, k, v, score, thr, bias)


def _top16_rows(s, payload=None):
    r = s.shape[0]
    iota = lax.broadcasted_iota(jnp.int32, s.shape, 0).astype(F32)
    vals, picks = [], []
    for _ in range(PEER_TOPK):
        m = jnp.max(s, axis=0, keepdims=True)
        am = jnp.min(jnp.where(s == m, iota, float(r)), axis=0, keepdims=True)
        hit = iota == am
        vals.append(m)
        if payload is None:
            picks.append(am)
        else:
            picks.append(jnp.sum(jnp.where(hit, payload, 0.0), axis=0, keepdims=True))
        s = jnp.where(hit, -jnp.inf, s)
    return jnp.concatenate(vals, axis=0), jnp.concatenate(picks, axis=0)


def _product_candidates(sv, si):
    half = PEER_TOPK // 2
    row = lax.broadcasted_iota(jnp.int32, (half, sv[0].shape[1]), 0)
    vals = [sv[0][0:1] + sv[1]]
    ids = [si[0][0:1] * float(N_KEYS) + si[1]]
    for a in range(1, half):
        keep = row < PEER_TOPK // (a + 1)
        vals.append(jnp.where(keep, sv[0][a:a + 1] + sv[1][0:half], -jnp.inf))
        ids.append(si[0][a:a + 1] * float(N_KEYS) + si[1][0:half])
    vals.append(sv[0][half:] + sv[1][0:1])
    ids.append(si[0][half:] * float(N_KEYS) + si[1][0:1])
    return jnp.concatenate(vals, axis=0), jnp.concatenate(ids, axis=0)


def _peer_select_kernel(q_ref, keys_ref, idx_ref, gate_ref):
    for h in range(PEER_HEADS):
        sv, si = [], []
        for c in range(2):
            hc = 2 * h + c
            s = _qk(keys_ref[hc], q_ref[:, hc * 128:(hc + 1) * 128])
            v, ix = _top16_rows(s)
            sv.append(v)
            si.append(ix)
        cand, cidx = _product_candidates(sv, si)
        fv, eidx = _top16_rows(cand, payload=cidx)
        e = jnp.exp(fv - fv[0:1])
        gate = e / jnp.sum(e, axis=0, keepdims=True)
        idx_ref[0, h * PEER_TOPK:(h + 1) * PEER_TOPK, :] = eidx.astype(jnp.int32)
        gate_ref[0, h * PEER_TOPK:(h + 1) * PEER_TOPK, :] = gate


def _peer_select_call(q, keys):
    n, dq = q.shape
    t = _pick(n, (PEER_SEL_TOKENS, 128))
    out = pl.BlockSpec((1, N_PICK, t), lambda i: (i, 0, 0))
    return pl.pallas_call(
        _peer_select_kernel,
        out_shape=(jax.ShapeDtypeStruct((n // t, N_PICK, t), jnp.int32),
                   jax.ShapeDtypeStruct((n // t, N_PICK, t), F32)),
        grid=(n // t,),
        in_specs=[pl.BlockSpec((t, dq), lambda i: (i, 0)),
                  pl.BlockSpec(keys.shape, lambda i: (0, 0, 0))],
        out_specs=(out, out),
        compiler_params=_cparams(("parallel",)),
        name="peer_select",
    )(q, keys)


def _peer_group(buf_ref, x_row, gate_col, r0):
    words = buf_ref[r0:r0 + 8, :]
    u = pltpu.bitcast(words & jnp.uint32(0xFFFF0000), F32)
    v = pltpu.bitcast(words << 16, F32)
    h = jnp.sum(u * x_row, axis=-1, keepdims=True)
    act = 0.5 * h * (1.0 + lax.erf(h * (2.0 ** -0.5)))
    return (gate_col * act) * v


def _peer_expert_kernel(idx_ref, x_ref, gate_ref, tab_ref, o_ref, buf0, buf1, sem, *, nt):
    i = pl.program_id(0)
    rows = PEER_TILE * N_PICK
    bufs = (buf0, buf1)

    def issue(parity, r):
        pltpu.make_async_copy(tab_ref.at[pl.ds(idx_ref[0, 0, r], 1), :], bufs[parity].at[pl.ds(r, 1), :],
                              sem.at[parity]).start(priority=r % 2)

    def compute(work, issue_parity):
        pltpu.make_async_copy(tab_ref.at[pl.ds(0, rows), :], bufs[work], sem.at[work]).wait()
        for t in range(PEER_TILE):
            x_row = x_ref[t:t + 1, :]
            acc = None
            for g in range(N_PICK // 8):
                r0 = t * N_PICK + g * 8
                if issue_parity is not None:
                    for r in range(r0, r0 + 8):
                        issue(issue_parity, r)
                c = _peer_group(bufs[work], x_row, gate_ref[0, g * 8:(g + 1) * 8, t:t + 1], r0)
                acc = c if acc is None else acc + c
            o_ref[t:t + 1, :] = jnp.sum(acc, axis=0, keepdims=True)

    @pl.when(i == 0)
    def _():
        for r in range(rows):
            issue(0, r)

    for parity in range(2):
        @pl.when((i > 0) & (i < nt) & (lax.rem(i, 2) == parity))
        def _():
            compute(1 - parity, parity)

    @pl.when(i == nt)
    def _():
        compute((nt - 1) % 2, None)


def _peer_expert_call(idx, x, gate, table):
    n, d = x.shape
    nt = n // PEER_TILE
    rows = PEER_TILE * N_PICK
    prev = lambda i: jnp.maximum(i - 1, 0)
    return pl.pallas_call(
        functools.partial(_peer_expert_kernel, nt=nt),
        out_shape=jax.ShapeDtypeStruct((n, d), F32),
        grid=(nt + 1,),
        in_specs=[pl.BlockSpec((1, 1, rows), lambda i: (jnp.minimum(i, nt - 1), 0, 0), memory_space=pltpu.SMEM),
                  pl.BlockSpec((PEER_TILE, d), lambda i: (prev(i), 0)),
                  pl.BlockSpec((1, N_PICK, PEER_TILE), lambda i: (prev(i), 0, 0)),
                  pl.BlockSpec(memory_space=pl.ANY)],
        out_specs=pl.BlockSpec((PEER_TILE, d), lambda i: (prev(i), 0)),
        scratch_shapes=[pltpu.VMEM((rows, d), jnp.uint32), pltpu.VMEM((rows, d), jnp.uint32),
                        pltpu.SemaphoreType.DMA((2,))],
        compiler_params=_cparams(("arbitrary",)),
        name="peer_experts",
    )(idx, x, gate, table)


def _rel_bucket(rel):
    half = N_BUCKETS // 2
    exact = half // 2
    base = jnp.where(rel > 0, half, 0)
    n = jnp.abs(rel)
    nf = jnp.maximum(n, 1).astype(F32)
    large = exact + (jnp.log(nf / exact) / math.log(MAX_DISTANCE / exact) * (half - exact)).astype(jnp.int32)
    large = jnp.minimum(large, half - 1)
    return base + jnp.where(n < exact, n, large)


def _bucket_saturation():
    half = N_BUCKETS // 2
    exact = half // 2
    n = np.arange(1, 1 << 16, dtype=np.float32)
    large = np.minimum(exact + (np.log(n / exact) / math.log(MAX_DISTANCE / exact) * (half - exact)).astype(np.int32),
                       half - 1)
    return int(np.max(np.nonzero(large < half - 1)[0]) + 2)


def _toeplitz(g, rows):
    lead, lg = g.shape[:-1], g.shape[-1]
    gp = jnp.pad(g, [(0, 0)] * len(lead) + [(0, 1)])
    flat = jnp.broadcast_to(gp[..., None, :], lead + (rows, lg + 1)).reshape(lead + (rows * (lg + 1),))
    skew = flat[..., :rows * lg].reshape(lead + (rows, lg))
    return skew[..., rows - 1:]


def _rel_values(table, lo, hi):
    return jnp.transpose(table[_rel_bucket(jnp.arange(lo, hi + 1))]).astype(F32)


def _mask_block(qpos, kpos, t_real, mask):
    ok = ((kpos >> CHUNK_SHIFT) <= (qpos >> CHUNK_SHIFT)) if mask == "chunk" else (kpos <= qpos)
    return jnp.where(ok & (kpos < t_real), 0.0, NEG).astype(F32)


def _bias_blocks(table, t, mask):
    assert t % CHUNK == 0 and t >= _bucket_saturation()
    diag = _mask_block(jnp.arange(t)[:, None], jnp.arange(t)[None, :], t, mask)
    if table is None:
        return diag[None, None]
    near = [_toeplitz(_rel_values(table, -(t - 1) - delta * t, (t - 1) - delta * t), t) for delta in range(2)]
    far = jnp.broadcast_to(_rel_values(table, -2 * t, -2 * t)[:, :, None], near[0].shape)
    return jnp.stack([near[0] + diag[None], near[1], far], axis=1)


def _bias_rows(table, p_len, t, t_k, t_real, mask):
    m = _mask_block(p_len + jnp.arange(t)[:, None], jnp.arange(t_k)[None, :], t_real, mask)
    if table is None:
        return m[None, None]
    return (_toeplitz(_rel_values(table, -(t - 1) - p_len, t_k - 1 - p_len), t) + m[None])[:, None]


def _in_proj_weights(w_in):
    a, b, c = H_A * HEAD_DIM, H_B * HEAD_DIM, H_C * HEAD_DIM
    sizes = (a, a, a, H_A, b, b, b, H_IDX * D_IDX, D_IDX, H_IDX, c, c, c)
    offs = np.concatenate([[0], np.cumsum(sizes)])
    col = lambda i: w_in[:, offs[i]:offs[i + 1]]
    qa, ka, va, fa, qb, kb, vb, iq, ik, iw, qc, kc, vc = [col(i) for i in range(13)]
    main = jnp.concatenate([qa, ka, va, qb, kb, vb, iq, qc, kc, vc], axis=1)
    tail = jnp.concatenate([ik, iw, fa], axis=1)
    tail = jnp.pad(tail, ((0, 0), (0, 128 - tail.shape[1])))
    return main.astype(BF16), tail.astype(BF16)


def _pack_tables(u_tab, v_tab):
    hi = lax.bitcast_convert_type(u_tab.astype(BF16), jnp.uint16).astype(jnp.uint32)
    lo = lax.bitcast_convert_type(v_tab.astype(BF16), jnp.uint16).astype(jnp.uint32)
    return (hi << 16) | lo


def _project(xb, w_main, w_tail, b_f):
    b, t, d = xb.shape
    x2 = xb.reshape(b * t, d)
    main = _matmul(x2, w_main)
    tail = _matmul(x2, w_tail)
    a, bb, c = H_A * HEAD_DIM, H_B * HEAD_DIM, H_C * HEAD_DIM
    sizes = (a, a, a, bb, bb, bb, H_IDX * D_IDX, c, c, c)
    offs = np.concatenate([[0], np.cumsum(sizes)])
    names = ("qa", "ka", "va", "qb", "kb", "vb", "iq", "qc", "kc", "vc")
    out = {nm: main[:, offs[i]:offs[i + 1]].reshape(b, t, -1) for i, nm in enumerate(names)}
    out["ik"] = tail[:, :D_IDX].reshape(b, t, D_IDX)
    out["iw"] = tail[:, D_IDX:D_IDX + H_IDX].reshape(b, t, H_IDX)
    fa = tail[:, D_IDX + H_IDX:D_IDX + H_IDX + H_A].reshape(b, t, H_A)
    out["logf"] = jax.nn.log_sigmoid(fa + b_f.astype(F32))
    return out


def _rows_out(p):
    b, t = p["ka"].shape[:2]
    hd = lambda a, n: a.reshape(b, t, n, HEAD_DIM)
    return (hd(p["ka"], H_A), hd(p["va"], H_A), p["logf"], hd(p["kb"], H_B), hd(p["vb"], H_B), p["ik"],
            hd(p["kc"], H_C), hd(p["vc"], H_C))


def _prompt_biases(rel_bias, t):
    blk = {nm: min(size, t) for nm, size in (("fox", FOX_BLOCK), ("dsa", ATT_BLOCK), ("diff", DIFF_BLOCK))}
    return dict(blk=blk, fox=_bias_blocks(None, blk["fox"], "causal"),
                dsa=_bias_blocks(rel_bias[:, :H_B], blk["dsa"], "chunk"),
                diff=_bias_blocks(rel_bias[:, H_B:], blk["diff"], "chunk"))


def _sample_biases(rel_bias, p_len, t):
    t_real = p_len + t
    t_k = -(-t_real // 128) * 128
    return dict(t_k=t_k, fox=_bias_rows(None, p_len, t, t_k, t_real, "causal"),
                dsa=_bias_rows(rel_bias[:, :H_B], p_len, t, t_k, t_real, "chunk"),
                diff=_bias_rows(rel_bias[:, H_B:], p_len, t, t_k, t_real, "chunk"))


def _mixer_prompt(xb, lw, lam, lam_init, biases):
    b, t, _ = xb.shape
    p = _project(xb, lw["w_main"], lw["w_tail"], lw["b_f"])
    blk = biases["blk"]
    topk = min(IDX_TOPK_MAX, t // 4)
    bf = lambda nm: p[nm].astype(BF16)
    dk = jnp.moveaxis(jnp.cumsum(p["logf"], axis=1), 2, 1)[:, :, None, :]
    oa = _flash_call(bf("qa"), bf("ka"), bf("va"), H_A, blk["fox"], blk["fox"], HEAD_DIM ** -0.5, True,
                     bias=biases["fox"], decay=dk)
    score, thr = _dsa_score_call(bf("iq"), p["iw"], bf("ik"), 0, t, topk, min(SCORE_ROWS, t))
    ob = _dsa_attn_call(bf("qb"), bf("kb"), bf("vb"), score, thr, biases["dsa"], blk["dsa"], blk["dsa"], True)
    oc = _diff_call(bf("qc"), bf("kc"), bf("vc"), biases["diff"], lam, lw["g_sub"], 1.0 - lam_init,
                    blk["diff"], blk["diff"], True)
    o = jnp.concatenate([oa, ob, oc], axis=-1).reshape(b * t, -1)
    return _matmul(o, lw["w_out"]), _rows_out(p)


def _mixer_sample(xb, caches, lw, lam, lam_init, biases):
    ca_k, ca_v, ca_logf, cb_k, cb_v, cb_ik, cc_k, cc_v = caches
    b, t, _ = xb.shape
    p_len = ca_k.shape[1]
    t_real = p_len + t
    t_k = biases["t_k"]
    p = _project(xb, lw["w_main"], lw["w_tail"], lw["b_f"])
    bf = lambda nm: p[nm].astype(BF16)

    def cat(cache, new):
        full = jnp.concatenate([cache.reshape(b, p_len, -1).astype(BF16), new.astype(BF16)], axis=1)
        return jnp.pad(full, ((0, 0), (0, t_k - t_real), (0, 0)))

    dcum = jnp.cumsum(jnp.concatenate([ca_logf.astype(F32), p["logf"]], axis=1), axis=1)
    dk = jnp.pad(jnp.moveaxis(dcum, 2, 1), ((0, 0), (0, 0), (0, t_k - t_real)))[:, :, None, :]
    topk = min(IDX_TOPK_MAX, t_real // 4)
    oa = _flash_call(bf("qa"), cat(ca_k, p["ka"]), cat(ca_v, p["va"]), H_A, t, t_k, HEAD_DIM ** -0.5, False,
                     bias=biases["fox"], decay=dk)
    score, thr = _dsa_score_call(bf("iq"), p["iw"], cat(cb_ik, p["ik"]), p_len, t_real, topk, t)
    ob = _dsa_attn_call(bf("qb"), cat(cb_k, p["kb"]), cat(cb_v, p["vb"]), score, thr, biases["dsa"], t, t_k, False)
    oc = _diff_call(bf("qc"), cat(cc_k, p["kc"]), cat(cc_v, p["vc"]), biases["diff"],
                    lam, lw["g_sub"], 1.0 - lam_init, t, t_k, False)
    o = jnp.concatenate([oa, ob, oc], axis=-1).reshape(b * t, -1)
    return _matmul(o, lw["w_out"]), _rows_out(p)


def _mem_attend(xb, mk, mv, lw):
    b, t, d = xb.shape
    q = _matmul(xb.reshape(b * t, d), lw["w_mq"]).astype(BF16).reshape(b, t, -1)
    tq = _pick(t, (ATT_BLOCK, 256, 128))
    o = _flash_call(q, mk, mv, H_MEM, tq, mk.shape[1], HEAD_DIM ** -0.5, False)
    return _matmul(o.reshape(b * t, -1), lw["w_mo"])


def _peer(x32, xb, lw):
    n, d = x32.shape
    q = _matmul(xb, lw["w_pq"]).astype(BF16)
    idx, gate = _peer_select_call(q, lw["sub_keys"])
    nsel, _, t = idx.shape
    idx = jnp.transpose(idx, (0, 2, 1)).reshape(n // PEER_TILE, 1, PEER_TILE * N_PICK)
    gate = gate.reshape(nsel, N_PICK, t // PEER_TILE, PEER_TILE)
    gate = jnp.transpose(gate, (0, 2, 1, 3)).reshape(n // PEER_TILE, N_PICK, PEER_TILE)
    return _peer_expert_call(idx, x32, gate, lw["table"])


def _lambda_value(lam_p, layer_idx):
    lam_init = 0.8 - 0.6 * math.exp(-0.3 * layer_idx)
    lp = lam_p.astype(F32)
    lam = jnp.exp(jnp.sum(lp[0] * lp[1])) - jnp.exp(jnp.sum(lp[2] * lp[3])) + lam_init
    return lam, lam_init


def kernel(x_prompt, x_sample, cache_a_k, cache_a_v, cache_a_logf, cache_b_k, cache_b_v, cache_b_ik, cache_c_k,
           cache_c_v, cache_mem_k, cache_mem_v, mem_prompt, ln_in_g, ln_in_b, w_in, b_f, w_out, diff_lambda,
           diff_subln, rel_bias, ln1_g, ln1_b, w_mq, w_mk, w_mv, w_mo, ln2_g, ln2_b, w_pq, sub_keys, u_tab,
           v_tab, ln3_g, ln3_b):
    depth = w_in.shape[0]
    alpha = (2 * depth) ** 0.25
    bp, tp, d = x_prompt.shape
    bs, ts, _ = x_sample.shape
    n_mem = mem_prompt.shape[1]

    xp32, xp16 = _ln_call(x_prompt.reshape(bp * tp, d), None, ln_in_g, ln_in_b, 1.0)
    xs32, xs16 = _ln_call(x_sample.reshape(bs * ts, d), None, ln_in_g, ln_in_b, 1.0)
    mem16 = mem_prompt.reshape(bp * n_mem, d).astype(BF16)
    bias_p = _prompt_biases(rel_bias, tp)
    bias_s = _sample_biases(rel_bias, cache_a_k.shape[2], ts)

    p_rows = [[] for _ in range(8)]
    s_rows = [[] for _ in range(8)]
    p_mk, p_mv = [], []
    for l in range(depth):
        lam, lam_init = _lambda_value(diff_lambda[l], l)
        w_main, w_tail = _in_proj_weights(w_in[l])
        lw = dict(w_main=w_main, w_tail=w_tail, b_f=b_f[l], w_out=w_out[l].astype(BF16), g_sub=diff_subln[l],
                  w_mq=w_mq[l].astype(BF16), w_mo=w_mo[l].astype(BF16), w_pq=w_pq[l].astype(BF16),
                  sub_keys=sub_keys[l].reshape(2 * PEER_HEADS, N_KEYS, -1).astype(BF16),
                  table=_pack_tables(u_tab[l], v_tab[l]))

        y, rows = _mixer_prompt(xp16.reshape(bp, tp, d), lw, lam, lam_init, bias_p)
        xp32, xp16 = _ln_call(xp32, y, ln1_g[l], ln1_b[l], alpha)
        mk = _matmul(mem16, w_mk[l].astype(BF16))
        mv = _matmul(mem16, w_mv[l].astype(BF16))
        y = _mem_attend(xp16.reshape(bp, tp, d), mk.astype(BF16).reshape(bp, n_mem, -1),
                        mv.astype(BF16).reshape(bp, n_mem, -1), lw)
        xp32, xp16 = _ln_call(xp32, y, ln2_g[l], ln2_b[l], alpha)
        y = _peer(xp32, xp16, lw)
        xp32, xp16 = _ln_call(xp32, y, ln3_g[l], ln3_b[l], alpha)
        for i in range(8):
            p_rows[i].append(rows[i])
        p_mk.append(mk.reshape(bp, n_mem, H_MEM, HEAD_DIM))
        p_mv.append(mv.reshape(bp, n_mem, H_MEM, HEAD_DIM))

        caches = (cache_a_k[l], cache_a_v[l], cache_a_logf[l], cache_b_k[l], cache_b_v[l], cache_b_ik[l],
                  cache_c_k[l], cache_c_v[l])
        y, rows = _mixer_sample(xs16.reshape(bs, ts, d), caches, lw, lam, lam_init, bias_s)
        xs32, xs16 = _ln_call(xs32, y, ln1_g[l], ln1_b[l], alpha)
        y = _mem_attend(xs16.reshape(bs, ts, d), cache_mem_k[l].reshape(bs, n_mem, -1).astype(BF16),
                        cache_mem_v[l].reshape(bs, n_mem, -1).astype(BF16), lw)
        xs32, xs16 = _ln_call(xs32, y, ln2_g[l], ln2_b[l], alpha)
        y = _peer(xs32, xs16, lw)
        xs32, xs16 = _ln_call(xs32, y, ln3_g[l], ln3_b[l], alpha)
        for i in range(8):
            s_rows[i].append(rows[i])

    ps = [jnp.stack(r) for r in p_rows]
    ss = [jnp.stack(r) for r in s_rows]
    return (xp32.reshape(bp, tp, d), xs32.reshape(bs, ts, d), ps[0], ps[1], ps[2], ps[3], ps[4], ps[5], ps[6],
            ps[7], jnp.stack(p_mk), jnp.stack(p_mv), ss[0], ss[1], ss[2], ss[3], ss[4], ss[5], ss[6], ss[7])
```

```python
import functools
import math

import jax
import jax.numpy as jnp
import numpy as np
from jax import lax
from jax.experimental import pallas as pl
from jax.experimental.pallas import tpu as pltpu

F32 = jnp.float32
BF16 = jnp.bfloat16

HEAD_DIM = 128
CHUNK = 64
CHUNK_SHIFT = 6
H_A, H_B, H_C = 6, 6, 4
DC = HEAD_DIM // 2
H_IDX, D_IDX = 16, 64
IDX_SCALE = (H_IDX * D_IDX) ** -0.5
IDX_TOPK_MAX = 256
N_BUCKETS = 32
MAX_DISTANCE = 256
H_MEM = 4
PEER_HEADS = 8
N_KEYS = 128
PEER_TOPK = 16
LN_EPS = 1e-5
NEG = -1e30

VMEM_LIMIT = 56 * 1024 * 1024
ATT_BLOCK = 512
FOX_BLOCK = 1024
DSA_BLOCK = 1024
DIFF_BLOCK = 1024
SCORE_ROWS = 128
SCORE_COLS = 512
PEER_SEL_TOKENS = 256
PEER_TILE = 8
N_PICK = PEER_HEADS * PEER_TOPK


def _cparams(sem):
    return pltpu.CompilerParams(dimension_semantics=sem, vmem_limit_bytes=VMEM_LIMIT)


def _ln_kernel(*refs, alpha, has_y):
    if has_y:
        x_ref, y_ref, g_ref, b_ref, o32_ref, o16_ref = refs
        z = alpha * x_ref[...] + y_ref[...]
    else:
        x_ref, g_ref, b_ref, o32_ref, o16_ref = refs
        z = x_ref[...]
    mu = jnp.mean(z, axis=-1, keepdims=True)
    zc = z - mu
    var = jnp.mean(zc * zc, axis=-1, keepdims=True)
    o = zc * lax.rsqrt(var + LN_EPS) * g_ref[...] + b_ref[...]
    o32_ref[...] = o
    o16_ref[...] = o.astype(BF16)


def _ln_call(x, y, g, b, alpha):
    n, d = x.shape
    tm = 256 if n % 256 == 0 else n
    row = pl.BlockSpec((tm, d), lambda i: (i, 0))
    vec = pl.BlockSpec((1, d), lambda i: (0, 0))
    has_y = y is not None
    args = (x, y) if has_y else (x,)
    return pl.pallas_call(
        functools.partial(_ln_kernel, alpha=alpha, has_y=has_y),
        out_shape=(jax.ShapeDtypeStruct((n, d), F32), jax.ShapeDtypeStruct((n, d), BF16)),
        grid=(n // tm,),
        in_specs=[row] * len(args) + [vec, vec],
        out_specs=(row, row),
        compiler_params=_cparams(("parallel",)),
        name="layer_norm",
    )(*args, g.reshape(1, d), b.reshape(1, d))


def _matmul_kernel(x_ref, w_ref, o_ref, *o16_ref):
    o = jnp.dot(x_ref[...], w_ref[...], preferred_element_type=F32)
    o_ref[...] = o
    if o16_ref:
        o16_ref[0][...] = o.astype(BF16)


def _pick(n, cands):
    for c in cands:
        if n % c == 0:
            return c
    return n


def _matmul(x, w, with_bf16=False):
    m, k = x.shape
    n = w.shape[1]
    tm = _pick(m, (512, 256, 128))
    tn = _pick(n, (1024, 512, 256, 128))
    out = pl.BlockSpec((tm, tn), lambda j, i: (i, j))
    shapes = [jax.ShapeDtypeStruct((m, n), F32)] + ([jax.ShapeDtypeStruct((m, n), BF16)] if with_bf16 else [])
    res = pl.pallas_call(
        _matmul_kernel,
        out_shape=tuple(shapes),
        grid=(n // tn, m // tm),
        in_specs=[pl.BlockSpec((tm, k), lambda j, i: (i, 0)),
                  pl.BlockSpec((k, tn), lambda j, i: (0, j))],
        out_specs=tuple([out] * len(shapes)),
        compiler_params=_cparams(("parallel", "parallel")),
        name="projection",
    )(x, w)
    return res if with_bf16 else res[0]


def _softmax_init(m_sc, l_sc, acc_sc):
    m_sc[...] = jnp.full(m_sc.shape, NEG, F32)
    l_sc[...] = jnp.zeros(l_sc.shape, F32)
    acc_sc[...] = jnp.zeros(acc_sc.shape, F32)


def _softmax_step(s, v, m_ref, l_ref, acc_ref):
    m_prev = m_ref[...]
    m_new = jnp.maximum(m_prev, jnp.max(s, axis=-1, keepdims=True))
    a = jnp.exp(m_prev - m_new)
    p = jnp.exp(s - m_new)
    l_ref[...] = a * l_ref[...] + jnp.sum(p, axis=-1, keepdims=True)
    acc_ref[...] = a * acc_ref[...] + jnp.dot(p.astype(BF16), v, preferred_element_type=F32)
    m_ref[...] = m_new


def _qk(q, k):
    return lax.dot_general(q, k, (((1,), (1,)), ((), ())), preferred_element_type=F32)


def _flash_kernel(*refs, scale, causal, nk, bias_mode, has_decay, has_sel):
    refs = list(refs)
    q_ref, k_ref, v_ref = refs[:3]
    pos = 3
    bias_ref = dk_ref = sel_ref = None
    if bias_mode is not None:
        bias_ref = refs[pos]
        pos += 1
    if has_decay:
        dk_ref = refs[pos]
        pos += 1
    if has_sel:
        sel_ref = refs[pos]
        pos += 1
    o_ref, m_sc, l_sc, acc_sc = refs[pos:pos + 4]
    i, j = pl.program_id(2), pl.program_id(3)
    last = i if causal else nk - 1

    @pl.when(j == 0)
    def _():
        _softmax_init(m_sc, l_sc, acc_sc)

    def step(with_bias):
        s = _qk(q_ref[0], k_ref[0]) * scale
        if has_decay:
            s = s - dk_ref[0, 0]
        if with_bias:
            s = s + bias_ref[0, 0]
        if has_sel:
            s = s + sel_ref[0].astype(F32)
        _softmax_step(s, v_ref[0], m_sc, l_sc, acc_sc)

    if causal and bias_mode == "diag":
        pl.when(j < last)(lambda: step(False))
        pl.when(j == last)(lambda: step(True))
    elif causal:
        pl.when(j <= last)(lambda: step(bias_mode is not None))
    else:
        step(bias_mode is not None)

    @pl.when(j == last)
    def _():
        o_ref[0] = (acc_sc[...] / l_sc[...]).astype(o_ref.dtype)


def _flash_call(q, k, v, n_heads, tq, tk, scale, causal, bias=None, bias_mode=None, decay=None, sel=None,
                offs=(0, 0, 0)):
    b, t_q, _ = q.shape
    t_k = k.shape[1]
    nq, nk = t_q // tq, t_k // tk
    d = HEAD_DIM
    qo, ko, vo = offs
    kj = (lambda i, j: jnp.minimum(j, i)) if causal else (lambda i, j: j)
    in_specs = [pl.BlockSpec((1, tq, d), lambda bb, h, i, j: (bb, i, qo + h)),
                pl.BlockSpec((1, tk, d), lambda bb, h, i, j: (bb, kj(i, j), ko + h)),
                pl.BlockSpec((1, tk, d), lambda bb, h, i, j: (bb, kj(i, j), vo + h))]
    args = [q, k, v]
    if bias is not None:
        assert bias_mode in (("diag", "blocks") if causal else ("full",))
        nb = bias.shape[1]
        hsel = (lambda h: h) if bias.shape[0] > 1 else (lambda h: 0)
        slot = (lambda i, j: jnp.clip(i - j, 0, nb - 1)) if bias_mode == "blocks" else (lambda i, j: 0)
        in_specs.append(pl.BlockSpec((1, 1, tq, tk), lambda bb, h, i, j: (hsel(h), slot(i, j), 0, 0)))
        args.append(bias)
    if decay is not None:
        in_specs.append(pl.BlockSpec((1, 1, 1, tk), lambda bb, h, i, j: (bb, h, 0, kj(i, j))))
        args.append(decay)
    if sel is not None:
        in_specs.append(pl.BlockSpec((1, tq, tk), lambda bb, h, i, j: (bb, i, kj(i, j))))
        args.append(sel)
    return pl.pallas_call(
        functools.partial(_flash_kernel, scale=scale, causal=causal, nk=nk, bias_mode=bias_mode,
                          has_decay=decay is not None, has_sel=sel is not None),
        out_shape=jax.ShapeDtypeStruct((b, t_q, n_heads * d), BF16),
        grid=(b, n_heads, nq, nk),
        in_specs=in_specs,
        out_specs=pl.BlockSpec((1, tq, d), lambda bb, h, i, j: (bb, i, h)),
        scratch_shapes=[pltpu.VMEM((tq, 1), F32), pltpu.VMEM((tq, 1), F32), pltpu.VMEM((tq, d), F32)],
        compiler_params=_cparams(("parallel", "parallel", "parallel", "arbitrary")),
        name="flash_attention",
    )(*args)


def _diff_kernel(lam_ref, q_ref, k_ref, v_ref, bias_ref, g_ref, o_ref, m_sc, l_sc, acc_sc, *, causal, nk, out_gain):
    i, j = pl.program_id(2), pl.program_id(3)
    last = i if causal else nk - 1

    @pl.when(j == 0)
    def _():
        _softmax_init(m_sc, l_sc, acc_sc)

    @pl.when(j <= last)
    def _():
        q, k, v = q_ref[0], k_ref[0], v_ref[0]
        bias = bias_ref[0, 0]
        for c in range(2):
            s = _qk(q[:, c * DC:(c + 1) * DC], k[:, c * DC:(c + 1) * DC]) * DC ** -0.5 + bias
            _softmax_step(s, v, m_sc.at[c], l_sc.at[c], acc_sc.at[c])

    @pl.when(j == last)
    def _():
        o = acc_sc[0] / l_sc[0] - lam_ref[0, 0] * (acc_sc[1] / l_sc[1])
        o = o * lax.rsqrt(jnp.mean(o * o, axis=-1, keepdims=True) + LN_EPS) * g_ref[...] * out_gain
        o_ref[0] = o.astype(o_ref.dtype)


def _diff_call(q, k, v, bias, lam, g_sub, out_gain, tq, tk, causal, offs=(0, 0, 0)):
    b, t_q, _ = q.shape
    t_k = k.shape[1]
    nq, nk = t_q // tq, t_k // tk
    d = HEAD_DIM
    nb = bias.shape[1]
    qo, ko, vo = offs
    kj = (lambda i, j: jnp.minimum(j, i)) if causal else (lambda i, j: j)
    slot = (lambda i, j: jnp.clip(i - j, 0, nb - 1)) if causal else (lambda i, j: 0)
    return pl.pallas_call(
        functools.partial(_diff_kernel, causal=causal, nk=nk, out_gain=out_gain),
        out_shape=jax.ShapeDtypeStruct((b, t_q, H_C * d), BF16),
        grid=(b, H_C, nq, nk),
        in_specs=[pl.BlockSpec(memory_space=pltpu.SMEM),
                  pl.BlockSpec((1, tq, d), lambda bb, h, i, j: (bb, i, qo + h)),
                  pl.BlockSpec((1, tk, d), lambda bb, h, i, j: (bb, kj(i, j), ko + h)),
                  pl.BlockSpec((1, tk, d), lambda bb, h, i, j: (bb, kj(i, j), vo + h)),
                  pl.BlockSpec((1, 1, tq, tk), lambda bb, h, i, j: (h, slot(i, j), 0, 0)),
                  pl.BlockSpec((1, d), lambda bb, h, i, j: (0, 0))],
        out_specs=pl.BlockSpec((1, tq, d), lambda bb, h, i, j: (bb, i, h)),
        scratch_shapes=[pltpu.VMEM((2, tq, 1), F32), pltpu.VMEM((2, tq, 1), F32), pltpu.VMEM((2, tq, d), F32)],
        compiler_params=_cparams(("parallel", "parallel", "parallel", "arbitrary")),
        name="diff_attention",
    )(lam.reshape(1, 1).astype(F32), q, k, v, bias, g_sub.reshape(1, d).astype(F32))


def _sort_key(x):
    bits = pltpu.bitcast(x, jnp.int32)
    return bits ^ ((bits >> 31) & jnp.int32(0x7FFFFFFF))


def _dsa_score_kernel(iq_ref, iw_ref, ik_ref, sel_ref, key_sc, *, q_off, t_real, topk, tkc):
    i = pl.program_id(1)
    tq, t_k = sel_ref.shape[1], sel_ref.shape[2]
    nck = t_k // tkc
    qpos = q_off + i * tq + lax.broadcasted_iota(jnp.int32, (tq, tkc), 0)
    col_end = jnp.minimum((((q_off + (i + 1) * tq - 1) >> CHUNK_SHIFT) + 1) * CHUNK, t_real)
    iq = iq_ref[0]
    w = iw_ref[0] * IDX_SCALE
    for c in range(nck):
        @pl.when(c * tkc < col_end)
        def _():
            ik = ik_ref[0, c * tkc:(c + 1) * tkc, :]
            acc = jnp.zeros((tq, tkc), F32)
            for h in range(H_IDX):
                sc = _qk(iq[:, h * D_IDX:(h + 1) * D_IDX], ik)
                acc = acc + w[:, h:h + 1] * jnp.maximum(sc, 0.0)
            kpos = c * tkc + lax.broadcasted_iota(jnp.int32, (tq, tkc), 1)
            adm = ((kpos >> CHUNK_SHIFT) <= (qpos >> CHUNK_SHIFT)) & (kpos < t_real)
            key_sc[c] = _sort_key(jnp.where(adm, acc + 0.0, -jnp.inf))

    n_need = (col_end + (tkc - 1)) >> (tkc.bit_length() - 1)

    def count_ge(cand):
        def body(c, acc):
            hit = jnp.where(key_sc[c] >= cand, 1.0, 0.0)
            for g in range(tkc // 128):
                acc = acc + hit[:, g * 128:(g + 1) * 128]
            return acc
        acc = lax.fori_loop(0, n_need, body, jnp.zeros((tq, 128), F32))
        return jnp.sum(acc, axis=-1, keepdims=True)

    int_min = jnp.int32(-2 ** 31)
    ans = jnp.full((tq, 1), int_min, jnp.int32)
    for bit in range(31, -1, -1):
        cand = (ans ^ int_min) if bit == 31 else (ans | jnp.int32(1 << bit))
        ans = jnp.where(count_ge(cand) >= float(topk), cand, ans)
    for c in range(nck):
        @pl.when(c * tkc < col_end)
        def _():
            sel_ref[0, :, c * tkc:(c + 1) * tkc] = jnp.where(key_sc[c] >= ans, 0.0, NEG).astype(BF16)

        @pl.when(c * tkc >= col_end)
        def _():
            sel_ref[0, :, c * tkc:(c + 1) * tkc] = jnp.full((tq, tkc), NEG, BF16)


def _dsa_select_call(iq, iw, ik, q_off, t_real, topk, tq):
    b, t_q, _ = iq.shape
    t_k = ik.shape[1]
    tkc = _pick(t_k, (SCORE_COLS, 128))
    assert tkc & (tkc - 1) == 0 and t_k % tkc == 0
    return pl.pallas_call(
        functools.partial(_dsa_score_kernel, q_off=q_off, t_real=t_real, topk=topk, tkc=tkc),
        out_shape=jax.ShapeDtypeStruct((b, t_q, t_k), BF16),
        grid=(b, t_q // tq),
        in_specs=[pl.BlockSpec((1, tq, H_IDX * D_IDX), lambda bb, i: (bb, i, 0)),
                  pl.BlockSpec((1, tq, H_IDX), lambda bb, i: (bb, i, 0)),
                  pl.BlockSpec((1, t_k, D_IDX), lambda bb, i: (bb, 0, 0))],
        out_specs=pl.BlockSpec((1, tq, t_k), lambda bb, i: (bb, i, 0)),
        scratch_shapes=[pltpu.VMEM((t_k // tkc, tq, tkc), jnp.int32)],
        compiler_params=_cparams(("parallel", "parallel")),
        name="dsa_indexer",
    )(iq, iw, ik)


def _top16_rows(s, payload=None):
    r = s.shape[0]
    iota = lax.broadcasted_iota(jnp.int32, s.shape, 0).astype(F32)
    vals, picks = [], []
    for _ in range(PEER_TOPK):
        m = jnp.max(s, axis=0, keepdims=True)
        am = jnp.min(jnp.where(s == m, iota, float(r)), axis=0, keepdims=True)
        hit = iota == am
        vals.append(m)
        if payload is None:
            picks.append(am)
        else:
            picks.append(jnp.sum(jnp.where(hit, payload, 0.0), axis=0, keepdims=True))
        s = jnp.where(hit, -jnp.inf, s)
    return jnp.concatenate(vals, axis=0), jnp.concatenate(picks, axis=0)


def _product_candidates(sv, si):
    half = PEER_TOPK // 2
    row = lax.broadcasted_iota(jnp.int32, (half, sv[0].shape[1]), 0)
    vals = [sv[0][0:1] + sv[1]]
    ids = [si[0][0:1] * float(N_KEYS) + si[1]]
    for a in range(1, half):
        keep = row < PEER_TOPK // (a + 1)
        vals.append(jnp.where(keep, sv[0][a:a + 1] + sv[1][0:half], -jnp.inf))
        ids.append(si[0][a:a + 1] * float(N_KEYS) + si[1][0:half])
    vals.append(sv[0][half:] + sv[1][0:1])
    ids.append(si[0][half:] * float(N_KEYS) + si[1][0:1])
    return jnp.concatenate(vals, axis=0), jnp.concatenate(ids, axis=0)


def _peer_select_kernel(q_ref, keys_ref, idx_ref, gate_ref):
    for h in range(PEER_HEADS):
        sv, si = [], []
        for c in range(2):
            hc = 2 * h + c
            s = _qk(keys_ref[hc], q_ref[:, hc * 128:(hc + 1) * 128])
            v, ix = _top16_rows(s)
            sv.append(v)
            si.append(ix)
        cand, cidx = _product_candidates(sv, si)
        fv, eidx = _top16_rows(cand, payload=cidx)
        e = jnp.exp(fv - fv[0:1])
        gate = e / jnp.sum(e, axis=0, keepdims=True)
        idx_ref[0, h * PEER_TOPK:(h + 1) * PEER_TOPK, :] = eidx.astype(jnp.int32)
        gate_ref[0, h * PEER_TOPK:(h + 1) * PEER_TOPK, :] = gate


def _peer_select_call(q, keys):
    n, dq = q.shape
    t = _pick(n, (PEER_SEL_TOKENS, 128))
    out = pl.BlockSpec((1, N_PICK, t), lambda i: (i, 0, 0))
    return pl.pallas_call(
        _peer_select_kernel,
        out_shape=(jax.ShapeDtypeStruct((n // t, N_PICK, t), jnp.int32),
                   jax.ShapeDtypeStruct((n // t, N_PICK, t), F32)),
        grid=(n // t,),
        in_specs=[pl.BlockSpec((t, dq), lambda i: (i, 0)),
                  pl.BlockSpec(keys.shape, lambda i: (0, 0, 0))],
        out_specs=(out, out),
        compiler_params=_cparams(("parallel",)),
        name="peer_select",
    )(q, keys)


def _peer_group(buf_ref, x_row, gate_col, r0):
    words = buf_ref[r0:r0 + 8, :]
    u = pltpu.bitcast(words & jnp.uint32(0xFFFF0000), F32)
    v = pltpu.bitcast(words << 16, F32)
    h = jnp.sum(u * x_row, axis=-1, keepdims=True)
    act = 0.5 * h * (1.0 + lax.erf(h * (2.0 ** -0.5)))
    return (gate_col * act) * v


def _peer_expert_kernel(idx_ref, x_ref, gate_ref, tab_ref, o_ref, buf0, buf1, sem, *, nt):
    i = pl.program_id(0)
    rows = PEER_TILE * N_PICK
    bufs = (buf0, buf1)

    def issue(parity, r):
        pltpu.make_async_copy(tab_ref.at[idx_ref[0, 0, r]], bufs[parity].at[pl.ds(r, 1), :],
                              sem.at[parity]).start(priority=r % 2)

    def compute(work, issue_parity):
        pltpu.make_async_copy(bufs[1 - work], bufs[work], sem.at[work]).wait()
        for t in range(PEER_TILE):
            x_row = x_ref[t:t + 1, :]
            acc = None
            for g in range(N_PICK // 8):
                r0 = t * N_PICK + g * 8
                if issue_parity is not None:
                    for r in range(r0, r0 + 8):
                        issue(issue_parity, r)
                c = _peer_group(bufs[work], x_row, gate_ref[0, g * 8:(g + 1) * 8, t:t + 1], r0)
                acc = c if acc is None else acc + c
            o_ref[t:t + 1, :] = jnp.sum(acc, axis=0, keepdims=True)

    @pl.when(i == 0)
    def _():
        for r in range(rows):
            issue(0, r)

    for parity in range(2):
        @pl.when((i > 0) & (i < nt) & (lax.rem(i, 2) == parity))
        def _():
            compute(1 - parity, parity)

    @pl.when(i == nt)
    def _():
        compute((nt - 1) % 2, None)


def _peer_expert_call(idx, x, gate, table):
    n, d = x.shape
    nt = n // PEER_TILE
    rows = PEER_TILE * N_PICK
    prev = lambda i: jnp.maximum(i - 1, 0)
    return pl.pallas_call(
        functools.partial(_peer_expert_kernel, nt=nt),
        out_shape=jax.ShapeDtypeStruct((n, d), F32),
        grid=(nt + 1,),
        in_specs=[pl.BlockSpec((1, 1, rows), lambda i: (jnp.minimum(i, nt - 1), 0, 0), memory_space=pltpu.SMEM),
                  pl.BlockSpec((PEER_TILE, d), lambda i: (prev(i), 0)),
                  pl.BlockSpec((1, N_PICK, PEER_TILE), lambda i: (prev(i), 0, 0)),
                  pl.BlockSpec(memory_space=pl.ANY)],
        out_specs=pl.BlockSpec((PEER_TILE, d), lambda i: (prev(i), 0)),
        scratch_shapes=[pltpu.VMEM((rows, d), jnp.uint32), pltpu.VMEM((rows, d), jnp.uint32),
                        pltpu.SemaphoreType.DMA((2,))],
        compiler_params=_cparams(("arbitrary",)),
        name="peer_experts",
    )(idx, x, gate, table.reshape(table.shape[0], 1, d))


def _bucket_saturation():
    half = N_BUCKETS // 2
    exact = half // 2
    n = np.arange(1, 1 << 16, dtype=np.float32)
    large = np.minimum(exact + (np.log(n / exact) / math.log(MAX_DISTANCE / exact) * (half - exact)).astype(np.int32),
                       half - 1)
    return int(np.max(np.nonzero(large < half - 1)[0]) + 2)


def _bucket_pieces(lo, hi):
    half = N_BUCKETS // 2
    exact = half // 2
    rel = np.arange(lo, hi + 1)
    n = np.abs(rel)
    steps = np.log(np.maximum(n, 1) / exact) / math.log(MAX_DISTANCE / exact) * (half - exact)
    frac = np.abs(steps - np.round(steps))
    fragile = (n > exact) & (exact + steps < half - 1) & (frac < 1e-4)
    assert not fragile.any()
    large = np.minimum(exact + np.floor(steps + 1e-9).astype(np.int64), half - 1)
    bucket = np.where(rel > 0, half, 0) + np.where(n < exact, n, large)
    starts = np.concatenate([[0], np.nonzero(np.diff(bucket))[0] + 1])
    return [(int(rel[s]), int(bucket[s])) for s in starts]


def _bias_of_rel(table, rel, lo, hi):
    tb = table.astype(F32)
    col = lambda b: tb[b].reshape((-1,) + (1,) * rel.ndim)
    pieces = _bucket_pieces(lo, hi)
    val = jnp.broadcast_to(col(pieces[0][1]), (tb.shape[1],) + rel.shape)
    for start, b in pieces[1:]:
        val = jnp.where(rel >= start, col(b), val)
    return val


def _mask_block(qpos, kpos, t_real, mask):
    ok = ((kpos >> CHUNK_SHIFT) <= (qpos >> CHUNK_SHIFT)) if mask == "chunk" else (kpos <= qpos)
    return jnp.where(ok & (kpos < t_real), 0.0, NEG).astype(F32)


def _bias_blocks(table, t, mask):
    assert t % CHUNK == 0 and t >= _bucket_saturation()
    qi, kj = jnp.arange(t)[:, None], jnp.arange(t)[None, :]
    diag = _mask_block(qi, kj, t, mask)
    if table is None:
        return diag[None, None]
    near = [_bias_of_rel(table, kj - qi - delta * t, -(t - 1) - delta * t, (t - 1) - delta * t) for delta in range(2)]
    far = jnp.broadcast_to(_bias_of_rel(table, jnp.full((1, 1), -2 * t), -2 * t, -2 * t), near[0].shape)
    return jnp.stack([near[0] + diag[None], near[1], far], axis=1)


def _bias_rows(table, p_len, t, t_k, t_real, mask):
    qpos, kpos = p_len + jnp.arange(t)[:, None], jnp.arange(t_k)[None, :]
    m = _mask_block(qpos, kpos, t_real, mask)
    if table is None:
        return m[None, None]
    return (_bias_of_rel(table, kpos - qpos, -(t - 1) - p_len, t_k - 1 - p_len) + m[None])[:, None]


def _in_proj_weights(w_in):
    a, b, c = H_A * HEAD_DIM, H_B * HEAD_DIM, H_C * HEAD_DIM
    sizes = (a, a, a, H_A, b, b, b, H_IDX * D_IDX, D_IDX, H_IDX, c, c, c)
    offs = np.concatenate([[0], np.cumsum(sizes)])
    col = lambda i: w_in[:, offs[i]:offs[i + 1]]
    qa, ka, va, fa, qb, kb, vb, iq, ik, iw, qc, kc, vc = [col(i) for i in range(13)]
    main = jnp.concatenate([iq, qa, ka, va, qb, kb, vb, qc, kc, vc], axis=1)
    tail = jnp.concatenate([ik, iw, fa], axis=1)
    tail = jnp.pad(tail, ((0, 0), (0, 128 - tail.shape[1])))
    return main.astype(BF16), tail.astype(BF16)


_MAIN_GROUPS = ("iq", "qa", "ka", "va", "qb", "kb", "vb", "qc", "kc", "vc")
_MAIN_WIDTHS = (H_IDX * D_IDX,) + (H_A * HEAD_DIM,) * 3 + (H_B * HEAD_DIM,) * 3 + (H_C * HEAD_DIM,) * 3
_MAIN_COL = dict(zip(_MAIN_GROUPS, np.concatenate([[0], np.cumsum(_MAIN_WIDTHS)[:-1]]).tolist()))
_MAIN_BLK = {nm: c // HEAD_DIM for nm, c in _MAIN_COL.items()}


def _pack_tables(u_tab, v_tab):
    hi = lax.bitcast_convert_type(u_tab.astype(BF16), jnp.uint16).astype(jnp.uint32)
    lo = lax.bitcast_convert_type(v_tab.astype(BF16), jnp.uint16).astype(jnp.uint32)
    return (hi << 16) | lo


def _project(xb, w_main, w_tail, b_f):
    b, t, d = xb.shape
    x2 = xb.reshape(b * t, d)
    main, main16 = _matmul(x2, w_main, with_bf16=True)
    tail = _matmul(x2, w_tail)
    out = {nm: main[:, _MAIN_COL[nm]:_MAIN_COL[nm] + w].reshape(b, t, -1)
           for nm, w in zip(_MAIN_GROUPS, _MAIN_WIDTHS) if nm[0] in "kv"}
    out["main16"] = main16.reshape(b, t, -1)
    out["ik"] = tail[:, :D_IDX].reshape(b, t, D_IDX)
    out["iw"] = tail[:, D_IDX:D_IDX + H_IDX].reshape(b, t, H_IDX)
    fa = tail[:, D_IDX + H_IDX:D_IDX + H_IDX + H_A].reshape(b, t, H_A)
    out["logf"] = jax.nn.log_sigmoid(fa + b_f.astype(F32))
    return out


def _rows_out(p):
    b, t = p["ka"].shape[:2]
    hd = lambda a, n: a.reshape(b, t, n, HEAD_DIM)
    return (hd(p["ka"], H_A), hd(p["va"], H_A), p["logf"], hd(p["kb"], H_B), hd(p["vb"], H_B), p["ik"],
            hd(p["kc"], H_C), hd(p["vc"], H_C))


def _prompt_biases(rel_bias, t):
    blk = {nm: min(size, t) for nm, size in (("fox", FOX_BLOCK), ("dsa", DSA_BLOCK), ("diff", DIFF_BLOCK))}
    return dict(blk=blk, fox=_bias_blocks(None, blk["fox"], "causal"),
                dsa=_bias_blocks(rel_bias[:, :H_B], blk["dsa"], "chunk"),
                diff=_bias_blocks(rel_bias[:, H_B:], blk["diff"], "chunk"))


def _sample_biases(rel_bias, p_len, t):
    t_real = p_len + t
    t_k = -(-t_real // 128) * 128
    return dict(t_k=t_k, fox=_bias_rows(None, p_len, t, t_k, t_real, "causal"),
                dsa=_bias_rows(rel_bias[:, :H_B], p_len, t, t_k, t_real, "chunk"),
                diff=_bias_rows(rel_bias[:, H_B:], p_len, t, t_k, t_real, "chunk"))


def _mixer_prompt(xb, lw, lam, lam_init, biases):
    b, t, _ = xb.shape
    p = _project(xb, lw["w_main"], lw["w_tail"], lw["b_f"])
    blk = biases["blk"]
    topk = min(IDX_TOPK_MAX, t // 4)
    m16 = p["main16"]
    grp = lambda *names: tuple(_MAIN_BLK[nm] for nm in names)
    dk = jnp.moveaxis(jnp.cumsum(p["logf"], axis=1), 2, 1)[:, :, None, :]
    oa = _flash_call(m16, m16, m16, H_A, blk["fox"], blk["fox"], HEAD_DIM ** -0.5, True,
                     bias=biases["fox"], bias_mode="diag", decay=dk, offs=grp("qa", "ka", "va"))
    sel = _dsa_select_call(m16, p["iw"], p["ik"].astype(BF16), 0, t, topk, min(SCORE_ROWS, t))
    ob = _flash_call(m16, m16, m16, H_B, blk["dsa"], blk["dsa"], HEAD_DIM ** -0.5, True,
                     bias=biases["dsa"], bias_mode="blocks", sel=sel, offs=grp("qb", "kb", "vb"))
    oc = _diff_call(m16, m16, m16, biases["diff"], lam, lw["g_sub"], 1.0 - lam_init,
                    blk["diff"], blk["diff"], True, offs=grp("qc", "kc", "vc"))
    o = jnp.concatenate([oa, ob, oc], axis=-1).reshape(b * t, -1)
    return _matmul(o, lw["w_out"]), _rows_out(p)


def _mixer_sample(xb, caches, lw, lam, lam_init, biases):
    ca_k, ca_v, ca_logf, cb_k, cb_v, cb_ik, cc_k, cc_v = caches
    b, t, _ = xb.shape
    p_len = ca_k.shape[1]
    t_real = p_len + t
    t_k = biases["t_k"]
    p = _project(xb, lw["w_main"], lw["w_tail"], lw["b_f"])
    m16 = p["main16"]

    def cat(cache, new):
        full = jnp.concatenate([cache.reshape(b, p_len, -1).astype(BF16), new.astype(BF16)], axis=1)
        return jnp.pad(full, ((0, 0), (0, t_k - t_real), (0, 0)))

    dcum = jnp.cumsum(jnp.concatenate([ca_logf.astype(F32), p["logf"]], axis=1), axis=1)
    dk = jnp.pad(jnp.moveaxis(dcum, 2, 1), ((0, 0), (0, 0), (0, t_k - t_real)))[:, :, None, :]
    topk = min(IDX_TOPK_MAX, t_real // 4)
    oa = _flash_call(m16, cat(ca_k, p["ka"]), cat(ca_v, p["va"]), H_A, t, t_k, HEAD_DIM ** -0.5, False,
                     bias=biases["fox"], bias_mode="full", decay=dk, offs=(_MAIN_BLK["qa"], 0, 0))
    sel = _dsa_select_call(m16, p["iw"], cat(cb_ik, p["ik"]), p_len, t_real, topk, t)
    ob = _flash_call(m16, cat(cb_k, p["kb"]), cat(cb_v, p["vb"]), H_B, t, t_k, HEAD_DIM ** -0.5, False,
                     bias=biases["dsa"], bias_mode="full", sel=sel, offs=(_MAIN_BLK["qb"], 0, 0))
    oc = _diff_call(m16, cat(cc_k, p["kc"]), cat(cc_v, p["vc"]), biases["diff"],
                    lam, lw["g_sub"], 1.0 - lam_init, t, t_k, False, offs=(_MAIN_BLK["qc"], 0, 0))
    o = jnp.concatenate([oa, ob, oc], axis=-1).reshape(b * t, -1)
    return _matmul(o, lw["w_out"]), _rows_out(p)


def _mem_attend(xb, mk, mv, lw):
    b, t, d = xb.shape
    q = _matmul(xb.reshape(b * t, d), lw["w_mq"]).astype(BF16).reshape(b, t, -1)
    tq = _pick(t, (ATT_BLOCK, 256, 128))
    o = _flash_call(q, mk, mv, H_MEM, tq, mk.shape[1], HEAD_DIM ** -0.5, False)
    return _matmul(o.reshape(b * t, -1), lw["w_mo"])


def _peer(x32, xb, lw):
    n, d = x32.shape
    q = _matmul(xb, lw["w_pq"]).astype(BF16)
    idx, gate = _peer_select_call(q, lw["sub_keys"])
    nsel, _, t = idx.shape
    idx = jnp.transpose(idx, (0, 2, 1)).reshape(n // PEER_TILE, 1, PEER_TILE * N_PICK)
    gate = gate.reshape(nsel, N_PICK, t // PEER_TILE, PEER_TILE)
    gate = jnp.transpose(gate, (0, 2, 1, 3)).reshape(n // PEER_TILE, N_PICK, PEER_TILE)
    return _peer_expert_call(idx, x32, gate, lw["table"])


def _lambda_value(lam_p, layer_idx):
    lam_init = 0.8 - 0.6 * math.exp(-0.3 * layer_idx)
    lp = lam_p.astype(F32)
    lam = jnp.exp(jnp.sum(lp[0] * lp[1])) - jnp.exp(jnp.sum(lp[2] * lp[3])) + lam_init
    return lam, lam_init


def kernel(x_prompt, x_sample, cache_a_k, cache_a_v, cache_a_logf, cache_b_k, cache_b_v, cache_b_ik, cache_c_k,
           cache_c_v, cache_mem_k, cache_mem_v, mem_prompt, ln_in_g, ln_in_b, w_in, b_f, w_out, diff_lambda,
           diff_subln, rel_bias, ln1_g, ln1_b, w_mq, w_mk, w_mv, w_mo, ln2_g, ln2_b, w_pq, sub_keys, u_tab,
           v_tab, ln3_g, ln3_b):
    depth = w_in.shape[0]
    alpha = (2 * depth) ** 0.25
    bp, tp, d = x_prompt.shape
    bs, ts, _ = x_sample.shape
    n_mem = mem_prompt.shape[1]

    xp32, xp16 = _ln_call(x_prompt.reshape(bp * tp, d), None, ln_in_g, ln_in_b, 1.0)
    xs32, xs16 = _ln_call(x_sample.reshape(bs * ts, d), None, ln_in_g, ln_in_b, 1.0)
    mem16 = mem_prompt.reshape(bp * n_mem, d).astype(BF16)
    bias_p = _prompt_biases(rel_bias, tp)
    bias_s = _sample_biases(rel_bias, cache_a_k.shape[2], ts)

    p_rows = [[] for _ in range(8)]
    s_rows = [[] for _ in range(8)]
    p_mk, p_mv = [], []
    for l in range(depth):
        lam, lam_init = _lambda_value(diff_lambda[l], l)
        w_main, w_tail = _in_proj_weights(w_in[l])
        lw = dict(w_main=w_main, w_tail=w_tail, b_f=b_f[l], w_out=w_out[l].astype(BF16), g_sub=diff_subln[l],
                  w_mq=w_mq[l].astype(BF16), w_mo=w_mo[l].astype(BF16), w_pq=w_pq[l].astype(BF16),
                  sub_keys=sub_keys[l].reshape(2 * PEER_HEADS, N_KEYS, -1).astype(BF16),
                  table=_pack_tables(u_tab[l], v_tab[l]))

        y, rows = _mixer_prompt(xp16.reshape(bp, tp, d), lw, lam, lam_init, bias_p)
        xp32, xp16 = _ln_call(xp32, y, ln1_g[l], ln1_b[l], alpha)
        mk = _matmul(mem16, w_mk[l].astype(BF16))
        mv = _matmul(mem16, w_mv[l].astype(BF16))
        y = _mem_attend(xp16.reshape(bp, tp, d), mk.astype(BF16).reshape(bp, n_mem, -1),
                        mv.astype(BF16).reshape(bp, n_mem, -1), lw)
        xp32, xp16 = _ln_call(xp32, y, ln2_g[l], ln2_b[l], alpha)
        y = _peer(xp32, xp16, lw)
        xp32, xp16 = _ln_call(xp32, y, ln3_g[l], ln3_b[l], alpha)
        for i in range(8):
            p_rows[i].append(rows[i])
        p_mk.append(mk.reshape(bp, n_mem, H_MEM, HEAD_DIM))
        p_mv.append(mv.reshape(bp, n_mem, H_MEM, HEAD_DIM))

        caches = (cache_a_k[l], cache_a_v[l], cache_a_logf[l], cache_b_k[l], cache_b_v[l], cache_b_ik[l],
                  cache_c_k[l], cache_c_v[l])
        y, rows = _mixer_sample(xs16.reshape(bs, ts, d), caches, lw, lam, lam_init, bias_s)
        xs32, xs16 = _ln_call(xs32, y, ln1_g[l], ln1_b[l], alpha)
        y = _mem_attend(xs16.reshape(bs, ts, d), cache_mem_k[l].reshape(bs, n_mem, -1).astype(BF16),
                        cache_mem_v[l].reshape(bs, n_mem, -1).astype(BF16), lw)
        xs32, xs16 = _ln_call(xs32, y, ln2_g[l], ln2_b[l], alpha)
        y = _peer(xs32, xs16, lw)
        xs32, xs16 = _ln_call(xs32, y, ln3_g[l], ln3_b[l], alpha)
        for i in range(8):
            s_rows[i].append(rows[i])

    ps = [jnp.stack(r) for r in p_rows]
    ss = [jnp.stack(r) for r in s_rows]
    return (xp32.reshape(bp, tp, d), xs32.reshape(bs, ts, d), ps[0], ps[1], ps[2], ps[3], ps[4], ps[5], ps[6],
            ps[7], jnp.stack(p_mk), jnp.stack(p_mv), ss[0], ss[1], ss[2], ss[3], ss[4], ss[5], ss[6], ss[7])
```

```python
import functools
import math

import jax
import jax.numpy as jnp
import numpy as np
from jax import lax
from jax.experimental import pallas as pl
from jax.experimental.pallas import tpu as pltpu

F32 = jnp.float32
BF16 = jnp.bfloat16

HEAD_DIM = 128
CHUNK = 64
CHUNK_SHIFT = 6
H_A, H_B, H_C = 6, 6, 4
DC = HEAD_DIM // 2
H_IDX, D_IDX = 16, 64
IDX_SCALE = (H_IDX * D_IDX) ** -0.5
IDX_TOPK_MAX = 256
N_BUCKETS = 32
MAX_DISTANCE = 256
H_MEM = 4
PEER_HEADS = 8
N_KEYS = 128
PEER_TOPK = 16
LN_EPS = 1e-5
NEG = -1e30

VMEM_LIMIT = 56 * 1024 * 1024
ATT_BLOCK = 512
FOX_BLOCK = 1024
DSA_BLOCK = 1024
DIFF_BLOCK = 1024
SCORE_ROWS = 128
SCORE_COLS = 512
PEER_SEL_TOKENS = 256
PEER_TILE = 8
N_PICK = PEER_HEADS * PEER_TOPK


def _cparams(sem):
    return pltpu.CompilerParams(dimension_semantics=sem, vmem_limit_bytes=VMEM_LIMIT)


def _ln_kernel(*refs, alpha, has_y):
    if has_y:
        x_ref, y_ref, g_ref, b_ref, o32_ref, o16_ref = refs
        z = alpha * x_ref[...] + y_ref[...]
    else:
        x_ref, g_ref, b_ref, o32_ref, o16_ref = refs
        z = x_ref[...]
    mu = jnp.mean(z, axis=-1, keepdims=True)
    zc = z - mu
    var = jnp.mean(zc * zc, axis=-1, keepdims=True)
    o = zc * lax.rsqrt(var + LN_EPS) * g_ref[...] + b_ref[...]
    o32_ref[...] = o
    o16_ref[...] = o.astype(BF16)


def _ln_call(x, y, g, b, alpha):
    n, d = x.shape
    tm = 256 if n % 256 == 0 else n
    row = pl.BlockSpec((tm, d), lambda i: (i, 0))
    vec = pl.BlockSpec((1, d), lambda i: (0, 0))
    has_y = y is not None
    args = (x, y) if has_y else (x,)
    return pl.pallas_call(
        functools.partial(_ln_kernel, alpha=alpha, has_y=has_y),
        out_shape=(jax.ShapeDtypeStruct((n, d), F32), jax.ShapeDtypeStruct((n, d), BF16)),
        grid=(n // tm,),
        in_specs=[row] * len(args) + [vec, vec],
        out_specs=(row, row),
        compiler_params=_cparams(("parallel",)),
        name="layer_norm",
    )(*args, g.reshape(1, d), b.reshape(1, d))


def _matmul_kernel(x_ref, w_ref, o_ref, *o16_ref):
    o = jnp.dot(x_ref[...], w_ref[...], preferred_element_type=F32)
    o_ref[...] = o
    if o16_ref:
        o16_ref[0][...] = o.astype(BF16)


def _pick(n, cands):
    for c in cands:
        if n % c == 0:
            return c
    return n


def _matmul(x, w, with_bf16=False):
    m, k = x.shape
    n = w.shape[1]
    tm = _pick(m, (512, 256, 128))
    tn = _pick(n, (1024, 512, 256, 128))
    out = pl.BlockSpec((tm, tn), lambda j, i: (i, j))
    shapes = [jax.ShapeDtypeStruct((m, n), F32)] + ([jax.ShapeDtypeStruct((m, n), BF16)] if with_bf16 else [])
    res = pl.pallas_call(
        _matmul_kernel,
        out_shape=tuple(shapes),
        grid=(n // tn, m // tm),
        in_specs=[pl.BlockSpec((tm, k), lambda j, i: (i, 0)),
                  pl.BlockSpec((k, tn), lambda j, i: (0, j))],
        out_specs=tuple([out] * len(shapes)),
        compiler_params=_cparams(("parallel", "parallel")),
        name="projection",
    )(x, w)
    return res if with_bf16 else res[0]


def _softmax_init(m_sc, l_sc, acc_sc):
    m_sc[...] = jnp.full(m_sc.shape, NEG, F32)
    l_sc[...] = jnp.zeros(l_sc.shape, F32)
    acc_sc[...] = jnp.zeros(acc_sc.shape, F32)


def _softmax_step(s, v, m_ref, l_ref, acc_ref):
    m_prev = m_ref[...]
    m_new = jnp.maximum(m_prev, jnp.max(s, axis=-1, keepdims=True))
    a = jnp.exp(m_prev - m_new)
    p = jnp.exp(s - m_new)
    l_ref[...] = a * l_ref[...] + jnp.sum(p, axis=-1, keepdims=True)
    acc_ref[...] = a * acc_ref[...] + jnp.dot(p.astype(BF16), v, preferred_element_type=F32)
    m_ref[...] = m_new


def _qk(q, k):
    return lax.dot_general(q, k, (((1,), (1,)), ((), ())), preferred_element_type=F32)


def _flash_kernel(*refs, scale, causal, nk, bias_mode, has_decay, has_sel, kv_head_major):
    refs = list(refs)
    q_ref, k_ref, v_ref = refs[:3]
    pos = 3
    bias_ref = dk_ref = sel_ref = None
    if bias_mode is not None:
        bias_ref = refs[pos]
        pos += 1
    if has_decay:
        dk_ref = refs[pos]
        pos += 1
    if has_sel:
        sel_ref = refs[pos]
        pos += 1
    o_ref, m_sc, l_sc, acc_sc = refs[pos:pos + 4]
    i, j = pl.program_id(2), pl.program_id(3)
    last = i if causal else nk - 1

    @pl.when(j == 0)
    def _():
        _softmax_init(m_sc, l_sc, acc_sc)

    def step(with_bias):
        kv = (lambda r: r[0, 0]) if kv_head_major else (lambda r: r[0])
        s = _qk(q_ref[0], kv(k_ref)) * scale
        if has_decay:
            s = s - dk_ref[0, 0]
        if with_bias:
            s = s + bias_ref[0, 0]
        if has_sel:
            s = s + sel_ref[0].astype(F32)
        _softmax_step(s, kv(v_ref), m_sc, l_sc, acc_sc)

    if causal and bias_mode == "diag":
        pl.when(j < last)(lambda: step(False))
        pl.when(j == last)(lambda: step(True))
    elif causal:
        pl.when(j <= last)(lambda: step(bias_mode is not None))
    else:
        step(bias_mode is not None)

    @pl.when(j == last)
    def _():
        o_ref[0] = (acc_sc[...] / l_sc[...]).astype(o_ref.dtype)


def _flash_call(q, k, v, n_heads, tq, tk, scale, causal, bias=None, bias_mode=None, decay=None, sel=None,
                offs=(0, 0, 0), kv_head_major=False):
    b, t_q, _ = q.shape
    t_k = k.shape[2] if kv_head_major else k.shape[1]
    nq, nk = t_q // tq, t_k // tk
    d = HEAD_DIM
    qo, ko, vo = offs
    kj = (lambda i, j: jnp.minimum(j, i)) if causal else (lambda i, j: j)
    if kv_head_major:
        kv_spec = lambda off: pl.BlockSpec((1, 1, tk, d), lambda bb, h, i, j: (bb, off + h, kj(i, j), 0))
    else:
        kv_spec = lambda off: pl.BlockSpec((1, tk, d), lambda bb, h, i, j: (bb, kj(i, j), off + h))
    in_specs = [pl.BlockSpec((1, tq, d), lambda bb, h, i, j: (bb, i, qo + h)), kv_spec(ko), kv_spec(vo)]
    args = [q, k, v]
    if bias is not None:
        assert bias_mode in (("diag", "blocks") if causal else ("full",))
        nb = bias.shape[1]
        hsel = (lambda h: h) if bias.shape[0] > 1 else (lambda h: 0)
        slot = (lambda i, j: jnp.clip(i - j, 0, nb - 1)) if bias_mode == "blocks" else (lambda i, j: 0)
        in_specs.append(pl.BlockSpec((1, 1, tq, tk), lambda bb, h, i, j: (hsel(h), slot(i, j), 0, 0)))
        args.append(bias)
    if decay is not None:
        in_specs.append(pl.BlockSpec((1, 1, 1, tk), lambda bb, h, i, j: (bb, h, 0, kj(i, j))))
        args.append(decay)
    if sel is not None:
        in_specs.append(pl.BlockSpec((1, tq, tk), lambda bb, h, i, j: (bb, i, kj(i, j))))
        args.append(sel)
    return pl.pallas_call(
        functools.partial(_flash_kernel, scale=scale, causal=causal, nk=nk, bias_mode=bias_mode,
                          has_decay=decay is not None, has_sel=sel is not None, kv_head_major=kv_head_major),
        out_shape=jax.ShapeDtypeStruct((b, t_q, n_heads * d), BF16),
        grid=(b, n_heads, nq, nk),
        in_specs=in_specs,
        out_specs=pl.BlockSpec((1, tq, d), lambda bb, h, i, j: (bb, i, h)),
        scratch_shapes=[pltpu.VMEM((tq, 1), F32), pltpu.VMEM((tq, 1), F32), pltpu.VMEM((tq, d), F32)],
        compiler_params=_cparams(("parallel", "parallel", "parallel", "arbitrary")),
        name="flash_attention",
    )(*args)


def _diff_kernel(lam_ref, q_ref, k_ref, v_ref, bias_ref, g_ref, o_ref, m_sc, l_sc, acc_sc, *, causal, nk, out_gain,
                 kv_head_major):
    i, j = pl.program_id(2), pl.program_id(3)
    last = i if causal else nk - 1

    @pl.when(j == 0)
    def _():
        _softmax_init(m_sc, l_sc, acc_sc)

    @pl.when(j <= last)
    def _():
        kv = (lambda r: r[0, 0]) if kv_head_major else (lambda r: r[0])
        q, k, v = q_ref[0], kv(k_ref), kv(v_ref)
        bias = bias_ref[0, 0]
        for c in range(2):
            s = _qk(q[:, c * DC:(c + 1) * DC], k[:, c * DC:(c + 1) * DC]) * DC ** -0.5 + bias
            _softmax_step(s, v, m_sc.at[c], l_sc.at[c], acc_sc.at[c])

    @pl.when(j == last)
    def _():
        o = acc_sc[0] / l_sc[0] - lam_ref[0, 0] * (acc_sc[1] / l_sc[1])
        o = o * lax.rsqrt(jnp.mean(o * o, axis=-1, keepdims=True) + LN_EPS) * g_ref[...] * out_gain
        o_ref[0] = o.astype(o_ref.dtype)


def _diff_call(q, k, v, bias, lam, g_sub, out_gain, tq, tk, causal, offs=(0, 0, 0), kv_head_major=False):
    b, t_q, _ = q.shape
    t_k = k.shape[2] if kv_head_major else k.shape[1]
    nq, nk = t_q // tq, t_k // tk
    d = HEAD_DIM
    nb = bias.shape[1]
    qo, ko, vo = offs
    kj = (lambda i, j: jnp.minimum(j, i)) if causal else (lambda i, j: j)
    slot = (lambda i, j: jnp.clip(i - j, 0, nb - 1)) if causal else (lambda i, j: 0)
    if kv_head_major:
        kv_spec = lambda off: pl.BlockSpec((1, 1, tk, d), lambda bb, h, i, j: (bb, off + h, kj(i, j), 0))
    else:
        kv_spec = lambda off: pl.BlockSpec((1, tk, d), lambda bb, h, i, j: (bb, kj(i, j), off + h))
    return pl.pallas_call(
        functools.partial(_diff_kernel, causal=causal, nk=nk, out_gain=out_gain, kv_head_major=kv_head_major),
        out_shape=jax.ShapeDtypeStruct((b, t_q, H_C * d), BF16),
        grid=(b, H_C, nq, nk),
        in_specs=[pl.BlockSpec(memory_space=pltpu.SMEM),
                  pl.BlockSpec((1, tq, d), lambda bb, h, i, j: (bb, i, qo + h)),
                  kv_spec(ko), kv_spec(vo),
                  pl.BlockSpec((1, 1, tq, tk), lambda bb, h, i, j: (h, slot(i, j), 0, 0)),
                  pl.BlockSpec((1, d), lambda bb, h, i, j: (0, 0))],
        out_specs=pl.BlockSpec((1, tq, d), lambda bb, h, i, j: (bb, i, h)),
        scratch_shapes=[pltpu.VMEM((2, tq, 1), F32), pltpu.VMEM((2, tq, 1), F32), pltpu.VMEM((2, tq, d), F32)],
        compiler_params=_cparams(("parallel", "parallel", "parallel", "arbitrary")),
        name="diff_attention",
    )(lam.reshape(1, 1).astype(F32), q, k, v, bias, g_sub.reshape(1, d).astype(F32))


def _sort_key(x):
    bits = pltpu.bitcast(x, jnp.int32)
    return bits ^ ((bits >> 31) & jnp.int32(0x7FFFFFFF))


def _dsa_score_kernel(iq_ref, iw_ref, ik_ref, sel_ref, key_sc, seen_sc, *, q_off, t_real, topk, tkc):
    i = pl.program_id(1)
    tq, t_k = sel_ref.shape[1], sel_ref.shape[2]
    nck = t_k // tkc
    qpos = q_off + i * tq + lax.broadcasted_iota(jnp.int32, (tq, tkc), 0)
    col_end = jnp.minimum((((q_off + (i + 1) * tq - 1) >> CHUNK_SHIFT) + 1) * CHUNK, t_real)
    iq = iq_ref[0]
    w = iw_ref[0] * IDX_SCALE
    for c in range(nck):
        @pl.when(c * tkc < col_end)
        def _():
            ik = ik_ref[0, c * tkc:(c + 1) * tkc, :]
            acc = jnp.zeros((tq, tkc), F32)
            for h in range(H_IDX):
                sc = _qk(iq[:, h * D_IDX:(h + 1) * D_IDX], ik)
                acc = acc + w[:, h:h + 1] * jnp.maximum(sc, 0.0)
            kpos = c * tkc + lax.broadcasted_iota(jnp.int32, (tq, tkc), 1)
            adm = ((kpos >> CHUNK_SHIFT) <= (qpos >> CHUNK_SHIFT)) & (kpos < t_real)
            key_sc[c] = _sort_key(jnp.where(adm, acc + 0.0, -jnp.inf))

    n_need = (col_end + (tkc - 1)) >> (tkc.bit_length() - 1)

    def count(hit_fn):
        def body(c, acc):
            hit = jnp.where(hit_fn(key_sc[c]), 1.0, 0.0)
            for g in range(tkc // 128):
                acc = acc + hit[:, g * 128:(g + 1) * 128]
            return acc
        acc = lax.fori_loop(0, n_need, body, jnp.zeros((tq, 128), F32))
        return jnp.sum(acc, axis=-1, keepdims=True)

    int_min = jnp.int32(-2 ** 31)
    ans = jnp.full((tq, 1), int_min, jnp.int32)
    for bit in range(31, -1, -1):
        cand = (ans ^ int_min) if bit == 31 else (ans | jnp.int32(1 << bit))
        ans = jnp.where(count(lambda key: key >= cand) >= float(topk), cand, ans)

    quota = float(topk) - count(lambda key: key > ans)
    tied_out = jnp.max(jnp.where(count(lambda key: key >= ans) > float(topk), 1.0, 0.0)) > 0.0
    seen_sc[...] = jnp.zeros(seen_sc.shape, F32)
    for c in range(nck):
        @pl.when((c * tkc < col_end) & jnp.logical_not(tied_out))
        def _():
            sel_ref[0, :, c * tkc:(c + 1) * tkc] = jnp.where(key_sc[c] >= ans, 0.0, NEG).astype(BF16)

        @pl.when((c * tkc < col_end) & tied_out)
        def _():
            key = key_sc[c]
            tie = key == ans
            upper = (lax.broadcasted_iota(jnp.int32, (tkc, tkc), 0) <= lax.broadcasted_iota(jnp.int32, (tkc, tkc), 1))
            rank = seen_sc[...] + jnp.dot(jnp.where(tie, 1.0, 0.0).astype(BF16), jnp.where(upper, 1.0, 0.0).astype(BF16),
                                          preferred_element_type=F32)
            keep = (key > ans) | (tie & (rank <= quota))
            sel_ref[0, :, c * tkc:(c + 1) * tkc] = jnp.where(keep, 0.0, NEG).astype(BF16)
            seen_sc[...] = rank[:, tkc - 1:tkc]

        @pl.when(c * tkc >= col_end)
        def _():
            sel_ref[0, :, c * tkc:(c + 1) * tkc] = jnp.full((tq, tkc), NEG, BF16)


def _dsa_select_call(iq, iw, ik, q_off, t_real, topk, tq):
    b, t_q, _ = iq.shape
    t_k = ik.shape[1]
    tkc = _pick(t_k, (SCORE_COLS, 128))
    assert tkc & (tkc - 1) == 0 and t_k % tkc == 0
    return pl.pallas_call(
        functools.partial(_dsa_score_kernel, q_off=q_off, t_real=t_real, topk=topk, tkc=tkc),
        out_shape=jax.ShapeDtypeStruct((b, t_q, t_k), BF16),
        grid=(b, t_q // tq),
        in_specs=[pl.BlockSpec((1, tq, H_IDX * D_IDX), lambda bb, i: (bb, i, 0)),
                  pl.BlockSpec((1, tq, H_IDX), lambda bb, i: (bb, i, 0)),
                  pl.BlockSpec((1, t_k, D_IDX), lambda bb, i: (bb, 0, 0))],
        out_specs=pl.BlockSpec((1, tq, t_k), lambda bb, i: (bb, i, 0)),
        scratch_shapes=[pltpu.VMEM((t_k // tkc, tq, tkc), jnp.int32), pltpu.VMEM((tq, 1), F32)],
        compiler_params=_cparams(("parallel", "parallel")),
        name="dsa_indexer",
    )(iq, iw, ik)


def _top16_rows(s, payload=None):
    r = s.shape[0]
    iota = lax.broadcasted_iota(jnp.int32, s.shape, 0).astype(F32)
    vals, picks = [], []
    for _ in range(PEER_TOPK):
        m = jnp.max(s, axis=0, keepdims=True)
        am = jnp.min(jnp.where(s == m, iota, float(r)), axis=0, keepdims=True)
        hit = iota == am
        vals.append(m)
        if payload is None:
            picks.append(am)
        else:
            picks.append(jnp.sum(jnp.where(hit, payload, 0.0), axis=0, keepdims=True))
        s = jnp.where(hit, -jnp.inf, s)
    return jnp.concatenate(vals, axis=0), jnp.concatenate(picks, axis=0)


def _product_candidates(sv, si):
    half = PEER_TOPK // 2
    row = lax.broadcasted_iota(jnp.int32, (half, sv[0].shape[1]), 0)
    vals = [sv[0][0:1] + sv[1]]
    ids = [si[0][0:1] * float(N_KEYS) + si[1]]
    for a in range(1, half):
        keep = row < PEER_TOPK // (a + 1)
        vals.append(jnp.where(keep, sv[0][a:a + 1] + sv[1][0:half], -jnp.inf))
        ids.append(si[0][a:a + 1] * float(N_KEYS) + si[1][0:half])
    vals.append(sv[0][half:] + sv[1][0:1])
    ids.append(si[0][half:] * float(N_KEYS) + si[1][0:1])
    return jnp.concatenate(vals, axis=0), jnp.concatenate(ids, axis=0)


def _peer_select_kernel(q_ref, keys_ref, idx_ref, gate_ref):
    for h in range(PEER_HEADS):
        sv, si = [], []
        for c in range(2):
            hc = 2 * h + c
            s = _qk(keys_ref[hc], q_ref[:, hc * 128:(hc + 1) * 128])
            v, ix = _top16_rows(s)
            sv.append(v)
            si.append(ix)
        cand, cidx = _product_candidates(sv, si)
        fv, eidx = _top16_rows(cand, payload=cidx)
        e = jnp.exp(fv - fv[0:1])
        gate = e / jnp.sum(e, axis=0, keepdims=True)
        idx_ref[0, h * PEER_TOPK:(h + 1) * PEER_TOPK, :] = eidx.astype(jnp.int32)
        gate_ref[0, h * PEER_TOPK:(h + 1) * PEER_TOPK, :] = gate


def _peer_select_call(q, keys):
    n, dq = q.shape
    t = _pick(n, (PEER_SEL_TOKENS, 128))
    out = pl.BlockSpec((1, N_PICK, t), lambda i: (i, 0, 0))
    return pl.pallas_call(
        _peer_select_kernel,
        out_shape=(jax.ShapeDtypeStruct((n // t, N_PICK, t), jnp.int32),
                   jax.ShapeDtypeStruct((n // t, N_PICK, t), F32)),
        grid=(n // t,),
        in_specs=[pl.BlockSpec((t, dq), lambda i: (i, 0)),
                  pl.BlockSpec(keys.shape, lambda i: (0, 0, 0))],
        out_specs=(out, out),
        compiler_params=_cparams(("parallel",)),
        name="peer_select",
    )(q, keys)


def _peer_group(buf_ref, x_row, gate_col, r0):
    words = buf_ref[r0:r0 + 8, :]
    u = pltpu.bitcast(words & jnp.uint32(0xFFFF0000), F32)
    v = pltpu.bitcast(words << 16, F32)
    h = jnp.sum(u * x_row, axis=-1, keepdims=True)
    act = 0.5 * h * (1.0 + lax.erf(h * (2.0 ** -0.5)))
    return (gate_col * act) * v


def _peer_expert_kernel(idx_ref, x_ref, gate_ref, tab_ref, o_ref, buf0, buf1, sem, *, nt):
    i = pl.program_id(0)
    rows = PEER_TILE * N_PICK
    bufs = (buf0, buf1)

    def issue(parity, r):
        pltpu.make_async_copy(tab_ref.at[idx_ref[0, 0, r]], bufs[parity].at[pl.ds(r, 1), :],
                              sem.at[parity]).start(priority=r % 2)

    def compute(work, issue_parity):
        pltpu.make_async_copy(bufs[1 - work], bufs[work], sem.at[work]).wait()
        for t in range(PEER_TILE):
            x_row = x_ref[t:t + 1, :]
            acc = None
            for g in range(N_PICK // 8):
                r0 = t * N_PICK + g * 8
                if issue_parity is not None:
                    for r in range(r0, r0 + 8):
                        issue(issue_parity, r)
                c = _peer_group(bufs[work], x_row, gate_ref[0, g * 8:(g + 1) * 8, t:t + 1], r0)
                acc = c if acc is None else acc + c
            o_ref[t:t + 1, :] = jnp.sum(acc, axis=0, keepdims=True)

    @pl.when(i == 0)
    def _():
        for r in range(rows):
            issue(0, r)

    for parity in range(2):
        @pl.when((i > 0) & (i < nt) & (lax.rem(i, 2) == parity))
        def _():
            compute(1 - parity, parity)

    @pl.when(i == nt)
    def _():
        compute((nt - 1) % 2, None)


def _peer_expert_call(idx, x, gate, table):
    n, d = x.shape
    nt = n // PEER_TILE
    rows = PEER_TILE * N_PICK
    prev = lambda i: jnp.maximum(i - 1, 0)
    return pl.pallas_call(
        functools.partial(_peer_expert_kernel, nt=nt),
        out_shape=jax.ShapeDtypeStruct((n, d), F32),
        grid=(nt + 1,),
        in_specs=[pl.BlockSpec((1, 1, rows), lambda i: (jnp.minimum(i, nt - 1), 0, 0), memory_space=pltpu.SMEM),
                  pl.BlockSpec((PEER_TILE, d), lambda i: (prev(i), 0)),
                  pl.BlockSpec((1, N_PICK, PEER_TILE), lambda i: (prev(i), 0, 0)),
                  pl.BlockSpec(memory_space=pl.ANY)],
        out_specs=pl.BlockSpec((PEER_TILE, d), lambda i: (prev(i), 0)),
        scratch_shapes=[pltpu.VMEM((rows, d), jnp.uint32), pltpu.VMEM((rows, d), jnp.uint32),
                        pltpu.SemaphoreType.DMA((2,))],
        compiler_params=_cparams(("arbitrary",)),
        name="peer_experts",
    )(idx, x, gate, table.reshape(table.shape[0], 1, d))


def _bucket_saturation():
    half = N_BUCKETS // 2
    exact = half // 2
    n = np.arange(1, 1 << 16, dtype=np.float32)
    large = np.minimum(exact + (np.log(n / exact) / math.log(MAX_DISTANCE / exact) * (half - exact)).astype(np.int32),
                       half - 1)
    return int(np.max(np.nonzero(large < half - 1)[0]) + 2)


def _bucket_pieces(lo, hi):
    half = N_BUCKETS // 2
    exact = half // 2
    rel = np.arange(lo, hi + 1)
    n = np.abs(rel)
    steps = np.log(np.maximum(n, 1) / exact) / math.log(MAX_DISTANCE / exact) * (half - exact)
    frac = np.abs(steps - np.round(steps))
    fragile = (n > exact) & (exact + steps < half - 1) & (frac < 1e-4)
    assert not fragile.any()
    large = np.minimum(exact + np.floor(steps + 1e-9).astype(np.int64), half - 1)
    bucket = np.where(rel > 0, half, 0) + np.where(n < exact, n, large)
    starts = np.concatenate([[0], np.nonzero(np.diff(bucket))[0] + 1])
    return [(int(rel[s]), int(bucket[s])) for s in starts]


def _bias_of_rel(table, rel, lo, hi):
    tb = table.astype(F32)
    col = lambda b: tb[b].reshape((-1,) + (1,) * rel.ndim)
    pieces = _bucket_pieces(lo, hi)
    val = jnp.broadcast_to(col(pieces[0][1]), (tb.shape[1],) + rel.shape)
    for start, b in pieces[1:]:
        val = jnp.where(rel >= start, col(b), val)
    return val


def _mask_block(qpos, kpos, t_real, mask):
    ok = ((kpos >> CHUNK_SHIFT) <= (qpos >> CHUNK_SHIFT)) if mask == "chunk" else (kpos <= qpos)
    return jnp.where(ok & (kpos < t_real), 0.0, NEG).astype(F32)


def _bias_blocks(table, t, mask):
    assert t % CHUNK == 0 and t >= _bucket_saturation()
    qi, kj = jnp.arange(t)[:, None], jnp.arange(t)[None, :]
    diag = _mask_block(qi, kj, t, mask)
    if table is None:
        return diag[None, None]
    near = [_bias_of_rel(table, kj - qi - delta * t, -(t - 1) - delta * t, (t - 1) - delta * t) for delta in range(2)]
    far = jnp.broadcast_to(_bias_of_rel(table, jnp.full((1, 1), -2 * t), -2 * t, -2 * t), near[0].shape)
    return jnp.stack([near[0] + diag[None], near[1], far], axis=1)


def _bias_rows(table, p_len, t, t_k, t_real, mask):
    qpos, kpos = p_len + jnp.arange(t)[:, None], jnp.arange(t_k)[None, :]
    m = _mask_block(qpos, kpos, t_real, mask)
    if table is None:
        return m[None, None]
    return (_bias_of_rel(table, kpos - qpos, -(t - 1) - p_len, t_k - 1 - p_len) + m[None])[:, None]


def _in_proj_weights(w_in):
    a, b, c = H_A * HEAD_DIM, H_B * HEAD_DIM, H_C * HEAD_DIM
    sizes = (a, a, a, H_A, b, b, b, H_IDX * D_IDX, D_IDX, H_IDX, c, c, c)
    offs = np.concatenate([[0], np.cumsum(sizes)])
    col = lambda i: w_in[:, offs[i]:offs[i + 1]]
    qa, ka, va, fa, qb, kb, vb, iq, ik, iw, qc, kc, vc = [col(i) for i in range(13)]
    main = jnp.concatenate([iq, qa, ka, va, qb, kb, vb, qc, kc, vc], axis=1)
    tail = jnp.concatenate([ik, iw, fa], axis=1)
    tail = jnp.pad(tail, ((0, 0), (0, 128 - tail.shape[1])))
    return main.astype(BF16), tail.astype(BF16)


_MAIN_GROUPS = ("iq", "qa", "ka", "va", "qb", "kb", "vb", "qc", "kc", "vc")
_MAIN_WIDTHS = (H_IDX * D_IDX,) + (H_A * HEAD_DIM,) * 3 + (H_B * HEAD_DIM,) * 3 + (H_C * HEAD_DIM,) * 3
_MAIN_COL = dict(zip(_MAIN_GROUPS, np.concatenate([[0], np.cumsum(_MAIN_WIDTHS)[:-1]]).tolist()))
_MAIN_BLK = {nm: c // HEAD_DIM for nm, c in _MAIN_COL.items()}


def _pack_tables(u_tab, v_tab):
    hi = lax.bitcast_convert_type(u_tab.astype(BF16), jnp.uint16).astype(jnp.uint32)
    lo = lax.bitcast_convert_type(v_tab.astype(BF16), jnp.uint16).astype(jnp.uint32)
    return (hi << 16) | lo


def _project(xb, w_main, w_tail, b_f):
    b, t, d = xb.shape
    x2 = xb.reshape(b * t, d)
    main, main16 = _matmul(x2, w_main, with_bf16=True)
    tail = _matmul(x2, w_tail)
    out = {nm: main[:, _MAIN_COL[nm]:_MAIN_COL[nm] + w].reshape(b, t, -1)
           for nm, w in zip(_MAIN_GROUPS, _MAIN_WIDTHS) if nm[0] in "kv"}
    out["main16"] = main16.reshape(b, t, -1)
    out["ik"] = tail[:, :D_IDX].reshape(b, t, D_IDX)
    out["iw"] = tail[:, D_IDX:D_IDX + H_IDX].reshape(b, t, H_IDX)
    fa = tail[:, D_IDX + H_IDX:D_IDX + H_IDX + H_A].reshape(b, t, H_A)
    out["logf"] = jax.nn.log_sigmoid(fa + b_f.astype(F32))
    return out


def _rows_out(p):
    b, t = p["ka"].shape[:2]
    hd = lambda a, n: a.reshape(b, t, n, HEAD_DIM)
    return (hd(p["ka"], H_A), hd(p["va"], H_A), p["logf"], hd(p["kb"], H_B), hd(p["vb"], H_B), p["ik"],
            hd(p["kc"], H_C), hd(p["vc"], H_C))


def _prompt_biases(rel_bias, t):
    blk = {nm: min(size, t) for nm, size in (("fox", FOX_BLOCK), ("dsa", DSA_BLOCK), ("diff", DIFF_BLOCK))}
    return dict(blk=blk, fox=_bias_blocks(None, blk["fox"], "causal"),
                dsa=_bias_blocks(rel_bias[:, :H_B], blk["dsa"], "chunk"),
                diff=_bias_blocks(rel_bias[:, H_B:], blk["diff"], "chunk"))


def _sample_biases(rel_bias, p_len, t):
    t_real = p_len + t
    t_k = -(-t_real // 128) * 128
    return dict(t_k=t_k, fox=_bias_rows(None, p_len, t, t_k, t_real, "causal"),
                dsa=_bias_rows(rel_bias[:, :H_B], p_len, t, t_k, t_real, "chunk"),
                diff=_bias_rows(rel_bias[:, H_B:], p_len, t, t_k, t_real, "chunk"))


def _mixer_prompt(xb, lw, lam, lam_init, biases):
    b, t, _ = xb.shape
    p = _project(xb, lw["w_main"], lw["w_tail"], lw["b_f"])
    blk = biases["blk"]
    topk = min(IDX_TOPK_MAX, t // 4)
    m16 = p["main16"]
    grp = lambda *names: tuple(_MAIN_BLK[nm] for nm in names)
    dk = jnp.moveaxis(jnp.cumsum(p["logf"], axis=1), 2, 1)[:, :, None, :]
    oa = _flash_call(m16, m16, m16, H_A, blk["fox"], blk["fox"], HEAD_DIM ** -0.5, True,
                     bias=biases["fox"], bias_mode="diag", decay=dk, offs=grp("qa", "ka", "va"))
    sel = _dsa_select_call(m16, p["iw"], p["ik"].astype(BF16), 0, t, topk, min(SCORE_ROWS, t))
    ob = _flash_call(m16, m16, m16, H_B, blk["dsa"], blk["dsa"], HEAD_DIM ** -0.5, True,
                     bias=biases["dsa"], bias_mode="blocks", sel=sel, offs=grp("qb", "kb", "vb"))
    oc = _diff_call(m16, m16, m16, biases["diff"], lam, lw["g_sub"], 1.0 - lam_init,
                    blk["diff"], blk["diff"], True, offs=grp("qc", "kc", "vc"))
    o = jnp.concatenate([oa, ob, oc], axis=-1).reshape(b * t, -1)
    return _matmul(o, lw["w_out"]), _rows_out(p)


def _mixer_sample(xb, caches, lw, lam, lam_init, biases):
    ca_k, ca_v, ca_logf, cb_k, cb_v, cb_ik, cc_k, cc_v = caches
    b, t, _ = xb.shape
    p_len = ca_k.shape[1]
    t_real = p_len + t
    t_k = biases["t_k"]
    p = _project(xb, lw["w_main"], lw["w_tail"], lw["b_f"])
    m16 = p["main16"]

    def cat(cache, new):
        full = jnp.concatenate([cache.reshape(b, p_len, -1).astype(BF16), new.astype(BF16)], axis=1)
        return jnp.pad(full, ((0, 0), (0, t_k - t_real), (0, 0)))

    def cat_heads(cache, new):
        nw = new.reshape(b, t, cache.shape[2], HEAD_DIM)
        full = jnp.concatenate([jnp.transpose(cache, (0, 2, 1, 3)).astype(BF16),
                                jnp.transpose(nw, (0, 2, 1, 3)).astype(BF16)], axis=2)
        return jnp.pad(full, ((0, 0), (0, 0), (0, t_k - t_real), (0, 0)))

    dcum = jnp.cumsum(jnp.concatenate([ca_logf.astype(F32), p["logf"]], axis=1), axis=1)
    dk = jnp.pad(jnp.moveaxis(dcum, 2, 1), ((0, 0), (0, 0), (0, t_k - t_real)))[:, :, None, :]
    topk = min(IDX_TOPK_MAX, t_real // 4)
    oa = _flash_call(m16, cat_heads(ca_k, p["ka"]), cat_heads(ca_v, p["va"]), H_A, t, t_k, HEAD_DIM ** -0.5, False,
                     bias=biases["fox"], bias_mode="full", decay=dk, offs=(_MAIN_BLK["qa"], 0, 0),
                     kv_head_major=True)
    sel = _dsa_select_call(m16, p["iw"], cat(cb_ik, p["ik"]), p_len, t_real, topk, t)
    ob = _flash_call(m16, cat_heads(cb_k, p["kb"]), cat_heads(cb_v, p["vb"]), H_B, t, t_k, HEAD_DIM ** -0.5, False,
                     bias=biases["dsa"], bias_mode="full", sel=sel, offs=(_MAIN_BLK["qb"], 0, 0),
                     kv_head_major=True)
    oc = _diff_call(m16, cat_heads(cc_k, p["kc"]), cat_heads(cc_v, p["vc"]), biases["diff"],
                    lam, lw["g_sub"], 1.0 - lam_init, t, t_k, False, offs=(_MAIN_BLK["qc"], 0, 0),
                    kv_head_major=True)
    o = jnp.concatenate([oa, ob, oc], axis=-1).reshape(b * t, -1)
    return _matmul(o, lw["w_out"]), _rows_out(p)


def _mem_attend(xb, mk, mv, lw):
    b, t, d = xb.shape
    q = _matmul(xb.reshape(b * t, d), lw["w_mq"]).astype(BF16).reshape(b, t, -1)
    tq = _pick(t, (ATT_BLOCK, 256, 128))
    o = _flash_call(q, mk, mv, H_MEM, tq, mk.shape[1], HEAD_DIM ** -0.5, False)
    return _matmul(o.reshape(b * t, -1), lw["w_mo"])


def _peer(x32, xb, lw):
    n, d = x32.shape
    q = _matmul(xb, lw["w_pq"]).astype(BF16)
    idx, gate = _peer_select_call(q, lw["sub_keys"])
    nsel, _, t = idx.shape
    idx = jnp.transpose(idx, (0, 2, 1)).reshape(n // PEER_TILE, 1, PEER_TILE * N_PICK)
    gate = gate.reshape(nsel, N_PICK, t // PEER_TILE, PEER_TILE)
    gate = jnp.transpose(gate, (0, 2, 1, 3)).reshape(n // PEER_TILE, N_PICK, PEER_TILE)
    return _peer_expert_call(idx, x32, gate, lw["table"])


def _lambda_value(lam_p, layer_idx):
    lam_init = 0.8 - 0.6 * math.exp(-0.3 * layer_idx)
    lp = lam_p.astype(F32)
    lam = jnp.exp(jnp.sum(lp[0] * lp[1])) - jnp.exp(jnp.sum(lp[2] * lp[3])) + lam_init
    return lam, lam_init


def kernel(x_prompt, x_sample, cache_a_k, cache_a_v, cache_a_logf, cache_b_k, cache_b_v, cache_b_ik, cache_c_k,
           cache_c_v, cache_mem_k, cache_mem_v, mem_prompt, ln_in_g, ln_in_b, w_in, b_f, w_out, diff_lambda,
           diff_subln, rel_bias, ln1_g, ln1_b, w_mq, w_mk, w_mv, w_mo, ln2_g, ln2_b, w_pq, sub_keys, u_tab,
           v_tab, ln3_g, ln3_b):
    depth = w_in.shape[0]
    alpha = (2 * depth) ** 0.25
    bp, tp, d = x_prompt.shape
    bs, ts, _ = x_sample.shape
    n_mem = mem_prompt.shape[1]

    xp32, xp16 = _ln_call(x_prompt.reshape(bp * tp, d), None, ln_in_g, ln_in_b, 1.0)
    xs32, xs16 = _ln_call(x_sample.reshape(bs * ts, d), None, ln_in_g, ln_in_b, 1.0)
    mem16 = mem_prompt.reshape(bp * n_mem, d).astype(BF16)
    bias_p = _prompt_biases(rel_bias, tp)
    bias_s = _sample_biases(rel_bias, cache_a_k.shape[2], ts)

    p_rows = [[] for _ in range(8)]
    s_rows = [[] for _ in range(8)]
    p_mk, p_mv = [], []
    for l in range(depth):
        lam, lam_init = _lambda_value(diff_lambda[l], l)
        w_main, w_tail = _in_proj_weights(w_in[l])
        lw = dict(w_main=w_main, w_tail=w_tail, b_f=b_f[l], w_out=w_out[l].astype(BF16), g_sub=diff_subln[l],
                  w_mq=w_mq[l].astype(BF16), w_mo=w_mo[l].astype(BF16), w_pq=w_pq[l].astype(BF16),
                  sub_keys=sub_keys[l].reshape(2 * PEER_HEADS, N_KEYS, -1).astype(BF16),
                  table=_pack_tables(u_tab[l], v_tab[l]))

        y, rows = _mixer_prompt(xp16.reshape(bp, tp, d), lw, lam, lam_init, bias_p)
        xp32, xp16 = _ln_call(xp32, y, ln1_g[l], ln1_b[l], alpha)
        mk = _matmul(mem16, w_mk[l].astype(BF16))
        mv = _matmul(mem16, w_mv[l].astype(BF16))
        y = _mem_attend(xp16.reshape(bp, tp, d), mk.astype(BF16).reshape(bp, n_mem, -1),
                        mv.astype(BF16).reshape(bp, n_mem, -1), lw)
        xp32, xp16 = _ln_call(xp32, y, ln2_g[l], ln2_b[l], alpha)
        y = _peer(xp32, xp16, lw)
        xp32, xp16 = _ln_call(xp32, y, ln3_g[l], ln3_b[l], alpha)
        for i in range(8):
            p_rows[i].append(rows[i])
        p_mk.append(mk.reshape(bp, n_mem, H_MEM, HEAD_DIM))
        p_mv.append(mv.reshape(bp, n_mem, H_MEM, HEAD_DIM))

        caches = (cache_a_k[l], cache_a_v[l], cache_a_logf[l], cache_b_k[l], cache_b_v[l], cache_b_ik[l],
                  cache_c_k[l], cache_c_v[l])
        y, rows = _mixer_sample(xs16.reshape(bs, ts, d), caches, lw, lam, lam_init, bias_s)
        xs32, xs16 = _ln_call(xs32, y, ln1_g[l], ln1_b[l], alpha)
        y = _mem_attend(xs16.reshape(bs, ts, d), cache_mem_k[l].reshape(bs, n_mem, -1).astype(BF16),
                        cache_mem_v[l].reshape(bs, n_mem, -1).astype(BF16), lw)
        xs32, xs16 = _ln_call(xs32, y, ln2_g[l], ln2_b[l], alpha)
        y = _peer(xs32, xs16, lw)
        xs32, xs16 = _ln_call(xs32, y, ln3_g[l], ln3_b[l], alpha)
        for i in range(8):
            s_rows[i].append(rows[i])

    ps = [jnp.stack(r) for r in p_rows]
    ss = [jnp.stack(r) for r in s_rows]
    return (xp32.reshape(bp, tp, d), xs32.reshape(bs, ts, d), ps[0], ps[1], ps[2], ps[3], ps[4], ps[5], ps[6],
            ps[7], jnp.stack(p_mk), jnp.stack(p_mv), ss[0], ss[1], ss[2], ss[3], ss[4], ss[5], ss[6], ss[7])
```

```python
import functools
import math

import jax
import jax.numpy as jnp
import numpy as np
from jax import lax
from jax.experimental import pallas as pl
from jax.experimental.pallas import tpu as pltpu

F32 = jnp.float32
BF16 = jnp.bfloat16

HEAD_DIM = 128
CHUNK = 64
CHUNK_SHIFT = 6
H_A, H_B, H_C = 6, 6, 4
DC = HEAD_DIM // 2
H_IDX, D_IDX = 16, 64
IDX_SCALE = (H_IDX * D_IDX) ** -0.5
IDX_TOPK_MAX = 256
N_BUCKETS = 32
MAX_DISTANCE = 256
H_MEM = 4
PEER_HEADS = 8
N_KEYS = 128
PEER_TOPK = 16
LN_EPS = 1e-5
NEG = -1e30

VMEM_LIMIT = 56 * 1024 * 1024
ATT_BLOCK = 512
FOX_BLOCK = 1024
DSA_BLOCK = 1024
DIFF_BLOCK = 1024
SCORE_ROWS = 256
SCORE_COLS = 512
PEER_SEL_TOKENS = 256
PEER_TILE = 8
N_PICK = PEER_HEADS * PEER_TOPK


def _cparams(sem):
    return pltpu.CompilerParams(dimension_semantics=sem, vmem_limit_bytes=VMEM_LIMIT)


def _ln_kernel(*refs, alpha, has_y):
    if has_y:
        x_ref, y_ref, g_ref, b_ref, o32_ref, o16_ref = refs
        z = alpha * x_ref[...] + y_ref[...]
    else:
        x_ref, g_ref, b_ref, o32_ref, o16_ref = refs
        z = x_ref[...]
    mu = jnp.mean(z, axis=-1, keepdims=True)
    zc = z - mu
    var = jnp.mean(zc * zc, axis=-1, keepdims=True)
    o = zc * lax.rsqrt(var + LN_EPS) * g_ref[...] + b_ref[...]
    o32_ref[...] = o
    o16_ref[...] = o.astype(BF16)


def _ln_call(x, y, g, b, alpha):
    n, d = x.shape
    tm = 256 if n % 256 == 0 else n
    row = pl.BlockSpec((tm, d), lambda i: (i, 0))
    vec = pl.BlockSpec((1, d), lambda i: (0, 0))
    has_y = y is not None
    args = (x, y) if has_y else (x,)
    return pl.pallas_call(
        functools.partial(_ln_kernel, alpha=alpha, has_y=has_y),
        out_shape=(jax.ShapeDtypeStruct((n, d), F32), jax.ShapeDtypeStruct((n, d), BF16)),
        grid=(n // tm,),
        in_specs=[row] * len(args) + [vec, vec],
        out_specs=(row, row),
        compiler_params=_cparams(("parallel",)),
        name="layer_norm",
    )(*args, g.reshape(1, d), b.reshape(1, d))


def _matmul_kernel(x_ref, w_ref, o_ref, *o16_ref):
    o = jnp.dot(x_ref[...], w_ref[...], preferred_element_type=F32)
    o_ref[...] = o
    if o16_ref:
        o16_ref[0][...] = o.astype(BF16)


def _pick(n, cands):
    for c in cands:
        if n % c == 0:
            return c
    return n


def _matmul(x, w, with_bf16=False):
    m, k = x.shape
    n = w.shape[1]
    tm = _pick(m, (512, 256, 128))
    tn = _pick(n, (1024, 512, 256, 128))
    out = pl.BlockSpec((tm, tn), lambda j, i: (i, j))
    shapes = [jax.ShapeDtypeStruct((m, n), F32)] + ([jax.ShapeDtypeStruct((m, n), BF16)] if with_bf16 else [])
    res = pl.pallas_call(
        _matmul_kernel,
        out_shape=tuple(shapes),
        grid=(n // tn, m // tm),
        in_specs=[pl.BlockSpec((tm, k), lambda j, i: (i, 0)),
                  pl.BlockSpec((k, tn), lambda j, i: (0, j))],
        out_specs=tuple([out] * len(shapes)),
        compiler_params=_cparams(("parallel", "parallel")),
        name="projection",
    )(x, w)
    return res if with_bf16 else res[0]


def _softmax_init(m_sc, l_sc, acc_sc):
    m_sc[...] = jnp.full(m_sc.shape, NEG, F32)
    l_sc[...] = jnp.zeros(l_sc.shape, F32)
    acc_sc[...] = jnp.zeros(acc_sc.shape, F32)


def _softmax_step(s, v, m_ref, l_ref, acc_ref):
    m_prev = m_ref[...]
    m_new = jnp.maximum(m_prev, jnp.max(s, axis=-1, keepdims=True))
    a = jnp.exp(m_prev - m_new)
    p = jnp.exp(s - m_new)
    l_ref[...] = a * l_ref[...] + jnp.sum(p, axis=-1, keepdims=True)
    acc_ref[...] = a * acc_ref[...] + jnp.dot(p.astype(BF16), v, preferred_element_type=F32)
    m_ref[...] = m_new


def _qk(q, k):
    return lax.dot_general(q, k, (((1,), (1,)), ((), ())), preferred_element_type=F32)


def _flash_kernel(*refs, scale, causal, nk, bias_mode, has_decay, has_sel, kv_head_major):
    refs = list(refs)
    q_ref, k_ref, v_ref = refs[:3]
    pos = 3
    bias_ref = dk_ref = sel_ref = None
    if bias_mode is not None:
        bias_ref = refs[pos]
        pos += 1
    if has_decay:
        dk_ref = refs[pos]
        pos += 1
    if has_sel:
        sel_ref = refs[pos]
        pos += 1
    o_ref, m_sc, l_sc, acc_sc = refs[pos:pos + 4]
    i, j = pl.program_id(2), pl.program_id(3)
    last = i if causal else nk - 1

    @pl.when(j == 0)
    def _():
        _softmax_init(m_sc, l_sc, acc_sc)

    def step(with_bias):
        kv = (lambda r: r[0, 0]) if kv_head_major else (lambda r: r[0])
        s = _qk(q_ref[0], kv(k_ref)) * scale
        if has_decay:
            s = s - dk_ref[0, 0]
        if with_bias:
            s = s + bias_ref[0, 0]
        if has_sel:
            s = s + sel_ref[0].astype(F32)
        _softmax_step(s, kv(v_ref), m_sc, l_sc, acc_sc)

    if causal and bias_mode == "diag":
        pl.when(j < last)(lambda: step(False))
        pl.when(j == last)(lambda: step(True))
    elif causal:
        pl.when(j <= last)(lambda: step(bias_mode is not None))
    else:
        step(bias_mode is not None)

    @pl.when(j == last)
    def _():
        o_ref[0] = (acc_sc[...] / l_sc[...]).astype(o_ref.dtype)


def _flash_call(q, k, v, n_heads, tq, tk, scale, causal, bias=None, bias_mode=None, decay=None, sel=None,
                offs=(0, 0, 0), kv_head_major=False):
    b, t_q, _ = q.shape
    t_k = k.shape[2] if kv_head_major else k.shape[1]
    nq, nk = t_q // tq, t_k // tk
    d = HEAD_DIM
    qo, ko, vo = offs
    kj = (lambda i, j: jnp.minimum(j, i)) if causal else (lambda i, j: j)
    if kv_head_major:
        kv_spec = lambda off: pl.BlockSpec((1, 1, tk, d), lambda bb, h, i, j: (bb, off + h, kj(i, j), 0))
    else:
        kv_spec = lambda off: pl.BlockSpec((1, tk, d), lambda bb, h, i, j: (bb, kj(i, j), off + h))
    in_specs = [pl.BlockSpec((1, tq, d), lambda bb, h, i, j: (bb, i, qo + h)), kv_spec(ko), kv_spec(vo)]
    args = [q, k, v]
    if bias is not None:
        assert bias_mode in (("diag", "blocks") if causal else ("full",))
        nb = bias.shape[1]
        hsel = (lambda h: h) if bias.shape[0] > 1 else (lambda h: 0)
        slot = (lambda i, j: jnp.clip(i - j, 0, nb - 1)) if bias_mode == "blocks" else (lambda i, j: 0)
        in_specs.append(pl.BlockSpec((1, 1, tq, tk), lambda bb, h, i, j: (hsel(h), slot(i, j), 0, 0)))
        args.append(bias)
    if decay is not None:
        in_specs.append(pl.BlockSpec((1, 1, 1, tk), lambda bb, h, i, j: (bb, h, 0, kj(i, j))))
        args.append(decay)
    if sel is not None:
        in_specs.append(pl.BlockSpec((1, tq, tk), lambda bb, h, i, j: (bb, i, kj(i, j))))
        args.append(sel)
    return pl.pallas_call(
        functools.partial(_flash_kernel, scale=scale, causal=causal, nk=nk, bias_mode=bias_mode,
                          has_decay=decay is not None, has_sel=sel is not None, kv_head_major=kv_head_major),
        out_shape=jax.ShapeDtypeStruct((b, t_q, n_heads * d), BF16),
        grid=(b, n_heads, nq, nk),
        in_specs=in_specs,
        out_specs=pl.BlockSpec((1, tq, d), lambda bb, h, i, j: (bb, i, h)),
        scratch_shapes=[pltpu.VMEM((tq, 1), F32), pltpu.VMEM((tq, 1), F32), pltpu.VMEM((tq, d), F32)],
        compiler_params=_cparams(("parallel", "parallel", "parallel", "arbitrary")),
        name="flash_attention",
    )(*args)


def _diff_kernel(lam_ref, q_ref, k_ref, v_ref, bias_ref, g_ref, o_ref, m_sc, l_sc, acc_sc, *, causal, nk, out_gain,
                 kv_head_major):
    i, j = pl.program_id(2), pl.program_id(3)
    last = i if causal else nk - 1

    @pl.when(j == 0)
    def _():
        _softmax_init(m_sc, l_sc, acc_sc)

    @pl.when(j <= last)
    def _():
        kv = (lambda r: r[0, 0]) if kv_head_major else (lambda r: r[0])
        q, k, v = q_ref[0], kv(k_ref), kv(v_ref)
        bias = bias_ref[0, 0]
        for c in range(2):
            s = _qk(q[:, c * DC:(c + 1) * DC], k[:, c * DC:(c + 1) * DC]) * DC ** -0.5 + bias
            _softmax_step(s, v, m_sc.at[c], l_sc.at[c], acc_sc.at[c])

    @pl.when(j == last)
    def _():
        o = acc_sc[0] / l_sc[0] - lam_ref[0, 0] * (acc_sc[1] / l_sc[1])
        o = o * lax.rsqrt(jnp.mean(o * o, axis=-1, keepdims=True) + LN_EPS) * g_ref[...] * out_gain
        o_ref[0] = o.astype(o_ref.dtype)


def _diff_call(q, k, v, bias, lam, g_sub, out_gain, tq, tk, causal, offs=(0, 0, 0), kv_head_major=False):
    b, t_q, _ = q.shape
    t_k = k.shape[2] if kv_head_major else k.shape[1]
    nq, nk = t_q // tq, t_k // tk
    d = HEAD_DIM
    nb = bias.shape[1]
    qo, ko, vo = offs
    kj = (lambda i, j: jnp.minimum(j, i)) if causal else (lambda i, j: j)
    slot = (lambda i, j: jnp.clip(i - j, 0, nb - 1)) if causal else (lambda i, j: 0)
    if kv_head_major:
        kv_spec = lambda off: pl.BlockSpec((1, 1, tk, d), lambda bb, h, i, j: (bb, off + h, kj(i, j), 0))
    else:
        kv_spec = lambda off: pl.BlockSpec((1, tk, d), lambda bb, h, i, j: (bb, kj(i, j), off + h))
    return pl.pallas_call(
        functools.partial(_diff_kernel, causal=causal, nk=nk, out_gain=out_gain, kv_head_major=kv_head_major),
        out_shape=jax.ShapeDtypeStruct((b, t_q, H_C * d), BF16),
        grid=(b, H_C, nq, nk),
        in_specs=[pl.BlockSpec(memory_space=pltpu.SMEM),
                  pl.BlockSpec((1, tq, d), lambda bb, h, i, j: (bb, i, qo + h)),
                  kv_spec(ko), kv_spec(vo),
                  pl.BlockSpec((1, 1, tq, tk), lambda bb, h, i, j: (h, slot(i, j), 0, 0)),
                  pl.BlockSpec((1, d), lambda bb, h, i, j: (0, 0))],
        out_specs=pl.BlockSpec((1, tq, d), lambda bb, h, i, j: (bb, i, h)),
        scratch_shapes=[pltpu.VMEM((2, tq, 1), F32), pltpu.VMEM((2, tq, 1), F32), pltpu.VMEM((2, tq, d), F32)],
        compiler_params=_cparams(("parallel", "parallel", "parallel", "arbitrary")),
        name="diff_attention",
    )(lam.reshape(1, 1).astype(F32), q, k, v, bias, g_sub.reshape(1, d).astype(F32))


def _sort_key(x):
    bits = pltpu.bitcast(x, jnp.int32)
    return bits ^ ((bits >> 31) & jnp.int32(0x7FFFFFFF))


def _dsa_score_kernel(iq_ref, iw_ref, ik_ref, sel_ref, key_sc, seen_sc, *, q_off, t_real, topk, tkc):
    i = pl.program_id(1)
    tq, t_k = sel_ref.shape[1], sel_ref.shape[2]
    nck = t_k // tkc
    qpos = q_off + i * tq + lax.broadcasted_iota(jnp.int32, (tq, tkc), 0)
    col_end = jnp.minimum((((q_off + (i + 1) * tq - 1) >> CHUNK_SHIFT) + 1) * CHUNK, t_real)
    iq = iq_ref[0]
    w = iw_ref[0] * IDX_SCALE
    for c in range(nck):
        @pl.when(c * tkc < col_end)
        def _():
            ik = ik_ref[0, c * tkc:(c + 1) * tkc, :]
            acc = jnp.zeros((tq, tkc), F32)
            for h in range(H_IDX):
                sc = _qk(iq[:, h * D_IDX:(h + 1) * D_IDX], ik)
                acc = acc + w[:, h:h + 1] * jnp.maximum(sc, 0.0)
            kpos = c * tkc + lax.broadcasted_iota(jnp.int32, (tq, tkc), 1)
            adm = ((kpos >> CHUNK_SHIFT) <= (qpos >> CHUNK_SHIFT)) & (kpos < t_real)
            key_sc[c] = _sort_key(jnp.where(adm, acc + 0.0, -jnp.inf))

    n_need = (col_end + (tkc - 1)) >> (tkc.bit_length() - 1)

    def count(hit_fn):
        def body(c, acc):
            hit = jnp.where(hit_fn(key_sc[c]), 1.0, 0.0)
            for g in range(tkc // 128):
                acc = acc + hit[:, g * 128:(g + 1) * 128]
            return acc
        acc = lax.fori_loop(0, n_need, body, jnp.zeros((tq, 128), F32))
        return jnp.sum(acc, axis=-1, keepdims=True)

    int_min = jnp.int32(-2 ** 31)
    ans = jnp.full((tq, 1), int_min, jnp.int32)
    for bit in range(31, -1, -1):
        cand = (ans ^ int_min) if bit == 31 else (ans | jnp.int32(1 << bit))
        ans = jnp.where(count(lambda key: key >= cand) >= float(topk), cand, ans)

    quota = float(topk) - count(lambda key: key > ans)
    tied_out = jnp.max(jnp.where(count(lambda key: key >= ans) > float(topk), 1.0, 0.0)) > 0.0
    seen_sc[...] = jnp.zeros(seen_sc.shape, F32)
    for c in range(nck):
        @pl.when((c * tkc < col_end) & jnp.logical_not(tied_out))
        def _():
            sel_ref[0, :, c * tkc:(c + 1) * tkc] = jnp.where(key_sc[c] >= ans, 0.0, NEG).astype(BF16)

        @pl.when((c * tkc < col_end) & tied_out)
        def _():
            key = key_sc[c]
            tie = key == ans
            upper = (lax.broadcasted_iota(jnp.int32, (tkc, tkc), 0) <= lax.broadcasted_iota(jnp.int32, (tkc, tkc), 1))
            rank = seen_sc[...] + jnp.dot(jnp.where(tie, 1.0, 0.0).astype(BF16), jnp.where(upper, 1.0, 0.0).astype(BF16),
                                          preferred_element_type=F32)
            keep = (key > ans) | (tie & (rank <= quota))
            sel_ref[0, :, c * tkc:(c + 1) * tkc] = jnp.where(keep, 0.0, NEG).astype(BF16)
            seen_sc[...] = rank[:, tkc - 1:tkc]

        @pl.when(c * tkc >= col_end)
        def _():
            sel_ref[0, :, c * tkc:(c + 1) * tkc] = jnp.full((tq, tkc), NEG, BF16)


def _dsa_select_call(iq, iw, ik, q_off, t_real, topk, tq):
    b, t_q, _ = iq.shape
    t_k = ik.shape[1]
    tkc = _pick(t_k, (SCORE_COLS, 128))
    assert tkc & (tkc - 1) == 0 and t_k % tkc == 0
    return pl.pallas_call(
        functools.partial(_dsa_score_kernel, q_off=q_off, t_real=t_real, topk=topk, tkc=tkc),
        out_shape=jax.ShapeDtypeStruct((b, t_q, t_k), BF16),
        grid=(b, t_q // tq),
        in_specs=[pl.BlockSpec((1, tq, H_IDX * D_IDX), lambda bb, i: (bb, i, 0)),
                  pl.BlockSpec((1, tq, H_IDX), lambda bb, i: (bb, i, 0)),
                  pl.BlockSpec((1, t_k, D_IDX), lambda bb, i: (bb, 0, 0))],
        out_specs=pl.BlockSpec((1, tq, t_k), lambda bb, i: (bb, i, 0)),
        scratch_shapes=[pltpu.VMEM((t_k // tkc, tq, tkc), jnp.int32), pltpu.VMEM((tq, 1), F32)],
        compiler_params=_cparams(("parallel", "parallel")),
        name="dsa_indexer",
    )(iq, iw, ik)


def _top16_rows(s, payload=None):
    r = s.shape[0]
    iota = lax.broadcasted_iota(jnp.int32, s.shape, 0).astype(F32)
    vals, picks = [], []
    for _ in range(PEER_TOPK):
        m = jnp.max(s, axis=0, keepdims=True)
        am = jnp.min(jnp.where(s == m, iota, float(r)), axis=0, keepdims=True)
        hit = iota == am
        vals.append(m)
        if payload is None:
            picks.append(am)
        else:
            picks.append(jnp.sum(jnp.where(hit, payload, 0.0), axis=0, keepdims=True))
        s = jnp.where(hit, -jnp.inf, s)
    return jnp.concatenate(vals, axis=0), jnp.concatenate(picks, axis=0)


def _product_candidates(sv, si):
    half = PEER_TOPK // 2
    row = lax.broadcasted_iota(jnp.int32, (half, sv[0].shape[1]), 0)
    vals = [sv[0][0:1] + sv[1]]
    ids = [si[0][0:1] * float(N_KEYS) + si[1]]
    for a in range(1, half):
        keep = row < PEER_TOPK // (a + 1)
        vals.append(jnp.where(keep, sv[0][a:a + 1] + sv[1][0:half], -jnp.inf))
        ids.append(si[0][a:a + 1] * float(N_KEYS) + si[1][0:half])
    vals.append(sv[0][half:] + sv[1][0:1])
    ids.append(si[0][half:] * float(N_KEYS) + si[1][0:1])
    return jnp.concatenate(vals, axis=0), jnp.concatenate(ids, axis=0)


def _peer_select_kernel(q_ref, keys_ref, idx_ref, gate_ref):
    for h in range(PEER_HEADS):
        sv, si = [], []
        for c in range(2):
            hc = 2 * h + c
            s = _qk(keys_ref[hc], q_ref[:, hc * 128:(hc + 1) * 128])
            v, ix = _top16_rows(s)
            sv.append(v)
            si.append(ix)
        cand, cidx = _product_candidates(sv, si)
        fv, eidx = _top16_rows(cand, payload=cidx)
        e = jnp.exp(fv - fv[0:1])
        gate = e / jnp.sum(e, axis=0, keepdims=True)
        idx_ref[0, h * PEER_TOPK:(h + 1) * PEER_TOPK, :] = eidx.astype(jnp.int32)
        gate_ref[0, h * PEER_TOPK:(h + 1) * PEER_TOPK, :] = gate


def _peer_select_call(q, keys):
    n, dq = q.shape
    t = _pick(n, (PEER_SEL_TOKENS, 128))
    out = pl.BlockSpec((1, N_PICK, t), lambda i: (i, 0, 0))
    return pl.pallas_call(
        _peer_select_kernel,
        out_shape=(jax.ShapeDtypeStruct((n // t, N_PICK, t), jnp.int32),
                   jax.ShapeDtypeStruct((n // t, N_PICK, t), F32)),
        grid=(n // t,),
        in_specs=[pl.BlockSpec((t, dq), lambda i: (i, 0)),
                  pl.BlockSpec(keys.shape, lambda i: (0, 0, 0))],
        out_specs=(out, out),
        compiler_params=_cparams(("parallel",)),
        name="peer_select",
    )(q, keys)


def _peer_group(buf_ref, x_row, gate_col, r0):
    words = buf_ref[r0:r0 + 8, :]
    u = pltpu.bitcast(words & jnp.uint32(0xFFFF0000), F32)
    v = pltpu.bitcast(words << 16, F32)
    h = jnp.sum(u * x_row, axis=-1, keepdims=True)
    act = 0.5 * h * (1.0 + lax.erf(h * (2.0 ** -0.5)))
    return (gate_col * act) * v


def _peer_expert_kernel(idx_ref, x_ref, gate_ref, tab_ref, o_ref, buf0, buf1, sem, *, nt):
    i = pl.program_id(0)
    rows = PEER_TILE * N_PICK
    bufs = (buf0, buf1)

    def issue(parity, r):
        pltpu.make_async_copy(tab_ref.at[idx_ref[0, 0, r]], bufs[parity].at[pl.ds(r, 1), :],
                              sem.at[parity]).start(priority=r % 2)

    def compute(work, issue_parity):
        pltpu.make_async_copy(bufs[1 - work], bufs[work], sem.at[work]).wait()
        for t in range(PEER_TILE):
            x_row = x_ref[t:t + 1, :]
            acc = None
            for g in range(N_PICK // 8):
                r0 = t * N_PICK + g * 8
                if issue_parity is not None:
                    for r in range(r0, r0 + 8):
                        issue(issue_parity, r)
                c = _peer_group(bufs[work], x_row, gate_ref[0, g * 8:(g + 1) * 8, t:t + 1], r0)
                acc = c if acc is None else acc + c
            o_ref[t:t + 1, :] = jnp.sum(acc, axis=0, keepdims=True)

    @pl.when(i == 0)
    def _():
        for r in range(rows):
            issue(0, r)

    for parity in range(2):
        @pl.when((i > 0) & (i < nt) & (lax.rem(i, 2) == parity))
        def _():
            compute(1 - parity, parity)

    @pl.when(i == nt)
    def _():
        compute((nt - 1) % 2, None)


def _peer_expert_call(idx, x, gate, table):
    n, d = x.shape
    nt = n // PEER_TILE
    rows = PEER_TILE * N_PICK
    prev = lambda i: jnp.maximum(i - 1, 0)
    return pl.pallas_call(
        functools.partial(_peer_expert_kernel, nt=nt),
        out_shape=jax.ShapeDtypeStruct((n, d), F32),
        grid=(nt + 1,),
        in_specs=[pl.BlockSpec((1, 1, rows), lambda i: (jnp.minimum(i, nt - 1), 0, 0), memory_space=pltpu.SMEM),
                  pl.BlockSpec((PEER_TILE, d), lambda i: (prev(i), 0)),
                  pl.BlockSpec((1, N_PICK, PEER_TILE), lambda i: (prev(i), 0, 0)),
                  pl.BlockSpec(memory_space=pl.ANY)],
        out_specs=pl.BlockSpec((PEER_TILE, d), lambda i: (prev(i), 0)),
        scratch_shapes=[pltpu.VMEM((rows, d), jnp.uint32), pltpu.VMEM((rows, d), jnp.uint32),
                        pltpu.SemaphoreType.DMA((2,))],
        compiler_params=_cparams(("arbitrary",)),
        name="peer_experts",
    )(idx, x, gate, table.reshape(table.shape[0], 1, d))


def _bucket_saturation():
    half = N_BUCKETS // 2
    exact = half // 2
    n = np.arange(1, 1 << 16, dtype=np.float32)
    large = np.minimum(exact + (np.log(n / exact) / math.log(MAX_DISTANCE / exact) * (half - exact)).astype(np.int32),
                       half - 1)
    return int(np.max(np.nonzero(large < half - 1)[0]) + 2)


def _bucket_pieces(lo, hi):
    half = N_BUCKETS // 2
    exact = half // 2
    rel = np.arange(lo, hi + 1)
    n = np.abs(rel)
    steps = np.log(np.maximum(n, 1) / exact) / math.log(MAX_DISTANCE / exact) * (half - exact)
    frac = np.abs(steps - np.round(steps))
    fragile = (n > exact) & (exact + steps < half - 1) & (frac < 1e-4)
    assert not fragile.any()
    large = np.minimum(exact + np.floor(steps + 1e-9).astype(np.int64), half - 1)
    bucket = np.where(rel > 0, half, 0) + np.where(n < exact, n, large)
    starts = np.concatenate([[0], np.nonzero(np.diff(bucket))[0] + 1])
    return [(int(rel[s]), int(bucket[s])) for s in starts]


def _bias_of_rel(table, rel, lo, hi):
    tb = table.astype(F32)
    col = lambda b: tb[b].reshape((-1,) + (1,) * rel.ndim)
    pieces = _bucket_pieces(lo, hi)
    val = jnp.broadcast_to(col(pieces[0][1]), (tb.shape[1],) + rel.shape)
    for start, b in pieces[1:]:
        val = jnp.where(rel >= start, col(b), val)
    return val


def _mask_block(qpos, kpos, t_real, mask):
    ok = ((kpos >> CHUNK_SHIFT) <= (qpos >> CHUNK_SHIFT)) if mask == "chunk" else (kpos <= qpos)
    return jnp.where(ok & (kpos < t_real), 0.0, NEG).astype(F32)


def _bias_blocks(table, t, mask):
    assert t % CHUNK == 0 and t >= _bucket_saturation()
    qi, kj = jnp.arange(t)[:, None], jnp.arange(t)[None, :]
    diag = _mask_block(qi, kj, t, mask)
    if table is None:
        return diag[None, None]
    near = [_bias_of_rel(table, kj - qi - delta * t, -(t - 1) - delta * t, (t - 1) - delta * t) for delta in range(2)]
    far = jnp.broadcast_to(_bias_of_rel(table, jnp.full((1, 1), -2 * t), -2 * t, -2 * t), near[0].shape)
    return jnp.stack([near[0] + diag[None], near[1], far], axis=1)


def _bias_rows(table, p_len, t, t_k, t_real, mask):
    qpos, kpos = p_len + jnp.arange(t)[:, None], jnp.arange(t_k)[None, :]
    m = _mask_block(qpos, kpos, t_real, mask)
    if table is None:
        return m[None, None]
    return (_bias_of_rel(table, kpos - qpos, -(t - 1) - p_len, t_k - 1 - p_len) + m[None])[:, None]


def _in_proj_weights(w_in):
    a, b, c = H_A * HEAD_DIM, H_B * HEAD_DIM, H_C * HEAD_DIM
    sizes = (a, a, a, H_A, b, b, b, H_IDX * D_IDX, D_IDX, H_IDX, c, c, c)
    offs = np.concatenate([[0], np.cumsum(sizes)])
    col = lambda i: w_in[:, offs[i]:offs[i + 1]]
    qa, ka, va, fa, qb, kb, vb, iq, ik, iw, qc, kc, vc = [col(i) for i in range(13)]
    main = jnp.concatenate([iq, qa, ka, va, qb, kb, vb, qc, kc, vc], axis=1)
    tail = jnp.concatenate([ik, iw, fa], axis=1)
    tail = jnp.pad(tail, ((0, 0), (0, 128 - tail.shape[1])))
    return main.astype(BF16), tail.astype(BF16)


_MAIN_GROUPS = ("iq", "qa", "ka", "va", "qb", "kb", "vb", "qc", "kc", "vc")
_MAIN_WIDTHS = (H_IDX * D_IDX,) + (H_A * HEAD_DIM,) * 3 + (H_B * HEAD_DIM,) * 3 + (H_C * HEAD_DIM,) * 3
_MAIN_COL = dict(zip(_MAIN_GROUPS, np.concatenate([[0], np.cumsum(_MAIN_WIDTHS)[:-1]]).tolist()))
_MAIN_BLK = {nm: c // HEAD_DIM for nm, c in _MAIN_COL.items()}


def _pack_tables(u_tab, v_tab):
    hi = lax.bitcast_convert_type(u_tab.astype(BF16), jnp.uint16).astype(jnp.uint32)
    lo = lax.bitcast_convert_type(v_tab.astype(BF16), jnp.uint16).astype(jnp.uint32)
    return (hi << 16) | lo


def _project(xb, w_main, w_tail, b_f):
    b, t, d = xb.shape
    x2 = xb.reshape(b * t, d)
    main, main16 = _matmul(x2, w_main, with_bf16=True)
    tail = _matmul(x2, w_tail)
    out = {nm: main[:, _MAIN_COL[nm]:_MAIN_COL[nm] + w].reshape(b, t, -1)
           for nm, w in zip(_MAIN_GROUPS, _MAIN_WIDTHS) if nm[0] in "kv"}
    out["main16"] = main16.reshape(b, t, -1)
    out["ik"] = tail[:, :D_IDX].reshape(b, t, D_IDX)
    out["iw"] = tail[:, D_IDX:D_IDX + H_IDX].reshape(b, t, H_IDX)
    fa = tail[:, D_IDX + H_IDX:D_IDX + H_IDX + H_A].reshape(b, t, H_A)
    out["logf"] = jax.nn.log_sigmoid(fa + b_f.astype(F32))
    return out


def _rows_out(p):
    b, t = p["ka"].shape[:2]
    hd = lambda a, n: a.reshape(b, t, n, HEAD_DIM)
    return (hd(p["ka"], H_A), hd(p["va"], H_A), p["logf"], hd(p["kb"], H_B), hd(p["vb"], H_B), p["ik"],
            hd(p["kc"], H_C), hd(p["vc"], H_C))


def _prompt_biases(rel_bias, t):
    blk = {nm: min(size, t) for nm, size in (("fox", FOX_BLOCK), ("dsa", DSA_BLOCK), ("diff", DIFF_BLOCK))}
    return dict(blk=blk, fox=_bias_blocks(None, blk["fox"], "causal"),
                dsa=_bias_blocks(rel_bias[:, :H_B], blk["dsa"], "chunk"),
                diff=_bias_blocks(rel_bias[:, H_B:], blk["diff"], "chunk"))


def _sample_biases(rel_bias, p_len, t):
    t_real = p_len + t
    t_k = -(-t_real // 128) * 128
    return dict(t_k=t_k, fox=_bias_rows(None, p_len, t, t_k, t_real, "causal"),
                dsa=_bias_rows(rel_bias[:, :H_B], p_len, t, t_k, t_real, "chunk"),
                diff=_bias_rows(rel_bias[:, H_B:], p_len, t, t_k, t_real, "chunk"))


def _mixer_prompt(xb, lw, lam, lam_init, biases):
    b, t, _ = xb.shape
    p = _project(xb, lw["w_main"], lw["w_tail"], lw["b_f"])
    blk = biases["blk"]
    topk = min(IDX_TOPK_MAX, t // 4)
    m16 = p["main16"]
    grp = lambda *names: tuple(_MAIN_BLK[nm] for nm in names)
    dk = jnp.moveaxis(jnp.cumsum(p["logf"], axis=1), 2, 1)[:, :, None, :]
    oa = _flash_call(m16, m16, m16, H_A, blk["fox"], blk["fox"], HEAD_DIM ** -0.5, True,
                     bias=biases["fox"], bias_mode="diag", decay=dk, offs=grp("qa", "ka", "va"))
    sel = _dsa_select_call(m16, p["iw"], p["ik"].astype(BF16), 0, t, topk, min(SCORE_ROWS, t))
    ob = _flash_call(m16, m16, m16, H_B, blk["dsa"], blk["dsa"], HEAD_DIM ** -0.5, True,
                     bias=biases["dsa"], bias_mode="blocks", sel=sel, offs=grp("qb", "kb", "vb"))
    oc = _diff_call(m16, m16, m16, biases["diff"], lam, lw["g_sub"], 1.0 - lam_init,
                    blk["diff"], blk["diff"], True, offs=grp("qc", "kc", "vc"))
    o = jnp.concatenate([oa, ob, oc], axis=-1).reshape(b * t, -1)
    return _matmul(o, lw["w_out"]), _rows_out(p)


def _mixer_sample(xb, caches, lw, lam, lam_init, biases):
    ca_k, ca_v, ca_logf, cb_k, cb_v, cb_ik, cc_k, cc_v = caches
    b, t, _ = xb.shape
    p_len = ca_k.shape[1]
    t_real = p_len + t
    t_k = biases["t_k"]
    p = _project(xb, lw["w_main"], lw["w_tail"], lw["b_f"])
    m16 = p["main16"]

    def cat(cache, new):
        full = jnp.concatenate([cache.reshape(b, p_len, -1).astype(BF16), new.astype(BF16)], axis=1)
        return jnp.pad(full, ((0, 0), (0, t_k - t_real), (0, 0)))

    def cat_heads(cache, new):
        nw = new.reshape(b, t, cache.shape[2], HEAD_DIM)
        full = jnp.concatenate([jnp.transpose(cache, (0, 2, 1, 3)).astype(BF16),
                                jnp.transpose(nw, (0, 2, 1, 3)).astype(BF16)], axis=2)
        return jnp.pad(full, ((0, 0), (0, 0), (0, t_k - t_real), (0, 0)))

    dcum = jnp.cumsum(jnp.concatenate([ca_logf.astype(F32), p["logf"]], axis=1), axis=1)
    dk = jnp.pad(jnp.moveaxis(dcum, 2, 1), ((0, 0), (0, 0), (0, t_k - t_real)))[:, :, None, :]
    topk = min(IDX_TOPK_MAX, t_real // 4)
    oa = _flash_call(m16, cat_heads(ca_k, p["ka"]), cat_heads(ca_v, p["va"]), H_A, t, t_k, HEAD_DIM ** -0.5, False,
                     bias=biases["fox"], bias_mode="full", decay=dk, offs=(_MAIN_BLK["qa"], 0, 0),
                     kv_head_major=True)
    sel = _dsa_select_call(m16, p["iw"], cat(cb_ik, p["ik"]), p_len, t_real, topk, t)
    ob = _flash_call(m16, cat_heads(cb_k, p["kb"]), cat_heads(cb_v, p["vb"]), H_B, t, t_k, HEAD_DIM ** -0.5, False,
                     bias=biases["dsa"], bias_mode="full", sel=sel, offs=(_MAIN_BLK["qb"], 0, 0),
                     kv_head_major=True)
    oc = _diff_call(m16, cat_heads(cc_k, p["kc"]), cat_heads(cc_v, p["vc"]), biases["diff"],
                    lam, lw["g_sub"], 1.0 - lam_init, t, t_k, False, offs=(_MAIN_BLK["qc"], 0, 0),
                    kv_head_major=True)
    o = jnp.concatenate([oa, ob, oc], axis=-1).reshape(b * t, -1)
    return _matmul(o, lw["w_out"]), _rows_out(p)


def _mem_attend(xb, mk, mv, lw):
    b, t, d = xb.shape
    q = _matmul(xb.reshape(b * t, d), lw["w_mq"]).astype(BF16).reshape(b, t, -1)
    tq = _pick(t, (ATT_BLOCK, 256, 128))
    o = _flash_call(q, mk, mv, H_MEM, tq, mk.shape[1], HEAD_DIM ** -0.5, False)
    return _matmul(o.reshape(b * t, -1), lw["w_mo"])


def _peer(x32, xb, lw):
    n, d = x32.shape
    q = _matmul(xb, lw["w_pq"]).astype(BF16)
    idx, gate = _peer_select_call(q, lw["sub_keys"])
    nsel, _, t = idx.shape
    idx = jnp.transpose(idx, (0, 2, 1)).reshape(n // PEER_TILE, 1, PEER_TILE * N_PICK)
    gate = gate.reshape(nsel, N_PICK, t // PEER_TILE, PEER_TILE)
    gate = jnp.transpose(gate, (0, 2, 1, 3)).reshape(n // PEER_TILE, N_PICK, PEER_TILE)
    return _peer_expert_call(idx, x32, gate, lw["table"])


def _lambda_value(lam_p, layer_idx):
    lam_init = 0.8 - 0.6 * math.exp(-0.3 * layer_idx)
    lp = lam_p.astype(F32)
    lam = jnp.exp(jnp.sum(lp[0] * lp[1])) - jnp.exp(jnp.sum(lp[2] * lp[3])) + lam_init
    return lam, lam_init


def kernel(x_prompt, x_sample, cache_a_k, cache_a_v, cache_a_logf, cache_b_k, cache_b_v, cache_b_ik, cache_c_k,
           cache_c_v, cache_mem_k, cache_mem_v, mem_prompt, ln_in_g, ln_in_b, w_in, b_f, w_out, diff_lambda,
           diff_subln, rel_bias, ln1_g, ln1_b, w_mq, w_mk, w_mv, w_mo, ln2_g, ln2_b, w_pq, sub_keys, u_tab,
           v_tab, ln3_g, ln3_b):
    depth = w_in.shape[0]
    alpha = (2 * depth) ** 0.25
    bp, tp, d = x_prompt.shape
    bs, ts, _ = x_sample.shape
    n_mem = mem_prompt.shape[1]

    xp32, xp16 = _ln_call(x_prompt.reshape(bp * tp, d), None, ln_in_g, ln_in_b, 1.0)
    xs32, xs16 = _ln_call(x_sample.reshape(bs * ts, d), None, ln_in_g, ln_in_b, 1.0)
    mem16 = mem_prompt.reshape(bp * n_mem, d).astype(BF16)
    bias_p = _prompt_biases(rel_bias, tp)
    bias_s = _sample_biases(rel_bias, cache_a_k.shape[2], ts)

    p_rows = [[] for _ in range(8)]
    s_rows = [[] for _ in range(8)]
    p_mk, p_mv = [], []
    for l in range(depth):
        lam, lam_init = _lambda_value(diff_lambda[l], l)
        w_main, w_tail = _in_proj_weights(w_in[l])
        lw = dict(w_main=w_main, w_tail=w_tail, b_f=b_f[l], w_out=w_out[l].astype(BF16), g_sub=diff_subln[l],
                  w_mq=w_mq[l].astype(BF16), w_mo=w_mo[l].astype(BF16), w_pq=w_pq[l].astype(BF16),
                  sub_keys=sub_keys[l].reshape(2 * PEER_HEADS, N_KEYS, -1).astype(BF16),
                  table=_pack_tables(u_tab[l], v_tab[l]))

        y, rows = _mixer_prompt(xp16.reshape(bp, tp, d), lw, lam, lam_init, bias_p)
        xp32, xp16 = _ln_call(xp32, y, ln1_g[l], ln1_b[l], alpha)
        mk = _matmul(mem16, w_mk[l].astype(BF16))
        mv = _matmul(mem16, w_mv[l].astype(BF16))
        y = _mem_attend(xp16.reshape(bp, tp, d), mk.astype(BF16).reshape(bp, n_mem, -1),
                        mv.astype(BF16).reshape(bp, n_mem, -1), lw)
        xp32, xp16 = _ln_call(xp32, y, ln2_g[l], ln2_b[l], alpha)
        y = _peer(xp32, xp16, lw)
        xp32, xp16 = _ln_call(xp32, y, ln3_g[l], ln3_b[l], alpha)
        for i in range(8):
            p_rows[i].append(rows[i])
        p_mk.append(mk.reshape(bp, n_mem, H_MEM, HEAD_DIM))
        p_mv.append(mv.reshape(bp, n_mem, H_MEM, HEAD_DIM))

        caches = (cache_a_k[l], cache_a_v[l], cache_a_logf[l], cache_b_k[l], cache_b_v[l], cache_b_ik[l],
                  cache_c_k[l], cache_c_v[l])
        y, rows = _mixer_sample(xs16.reshape(bs, ts, d), caches, lw, lam, lam_init, bias_s)
        xs32, xs16 = _ln_call(xs32, y, ln1_g[l], ln1_b[l], alpha)
        y = _mem_attend(xs16.reshape(bs, ts, d), cache_mem_k[l].reshape(bs, n_mem, -1).astype(BF16),
                        cache_mem_v[l].reshape(bs, n_mem, -1).astype(BF16), lw)
        xs32, xs16 = _ln_call(xs32, y, ln2_g[l], ln2_b[l], alpha)
        y = _peer(xs32, xs16, lw)
        xs32, xs16 = _ln_call(xs32, y, ln3_g[l], ln3_b[l], alpha)
        for i in range(8):
            s_rows[i].append(rows[i])

    ps = [jnp.stack(r) for r in p_rows]
    ss = [jnp.stack(r) for r in s_rows]
    return (xp32.reshape(bp, tp, d), xs32.reshape(bs, ts, d), ps[0], ps[1], ps[2], ps[3], ps[4], ps[5], ps[6],
            ps[7], jnp.stack(p_mk), jnp.stack(p_mv), ss[0], ss[1], ss[2], ss[3], ss[4], ss[5], ss[6], ss[7])
```
